```python
import math
import jax, jax.numpy as jnp
from jax import lax
import numpy as np

D_MODEL = 1024
BATCH = 16
SEQ = 256
DEPTH = 1
DEC_BATCH = 4
DEC_SEQ = 1024
PAST_LEN = 512

GRID_W = 64
H_Q = 8
H_KV = 2
HD = 128
ROPE_F = HD // 4
ROPE_THETA = 10000.0
Q_BLOCK = 128
D_INNER = 2 * D_MODEL
SSD_P = 64
H_SSD = D_INNER // SSD_P
SSD_G = 8
SSD_N = 128
CONV_K = 5
CHUNK = 128
D_FF = 4 * D_MODEL
ALPHA = (2.0 * DEPTH) ** 0.25
BETA = (8.0 * DEPTH) ** -0.25
EPS = 1e-6
Q_DIM = H_Q * HD
KV_DIM = H_KV * HD
BC_DIM = SSD_G * SSD_N
XBC_DIM = D_INNER + 2 * BC_DIM
DT_DIM = 2 * H_SSD
GATE_DIM = 2 * D_MODEL
IN_DIM = Q_DIM + 2 * KV_DIM + D_INNER + XBC_DIM + DT_DIM + GATE_DIM
SPLIT_IDX = [Q_DIM, Q_DIM + KV_DIM, Q_DIM + 2 * KV_DIM, Q_DIM + 2 * KV_DIM + D_INNER,
             Q_DIM + 2 * KV_DIM + D_INNER + XBC_DIM, Q_DIM + 2 * KV_DIM + D_INNER + XBC_DIM + DT_DIM]

kernel_name = "hybrid_gqa_ssd_diffusion_prefix_step"


def layer_norm(x, g=None, b=None):
    xf = x.astype(jnp.float32)
    mu = jnp.mean(xf, axis=-1, keepdims=True)
    var = jnp.mean(jnp.square(xf - mu), axis=-1, keepdims=True)
    y = (xf - mu) * lax.rsqrt(var + EPS)
    if g is not None:
        y = y * g.astype(jnp.float32) + b.astype(jnp.float32)
    return y.astype(x.dtype)


def rms_norm(x, g):
    xf = x.astype(jnp.float32)
    y = xf * lax.rsqrt(jnp.mean(jnp.square(xf), axis=-1, keepdims=True) + EPS)
    return (y * g.astype(jnp.float32)).astype(x.dtype)


def rope_tables(L):
    rows = L // GRID_W
    t = jnp.arange(rows * GRID_W)
    pos = jnp.stack([t // GRID_W, t % GRID_W], axis=-1).astype(jnp.float32)
    inv = ROPE_THETA ** (-jnp.arange(ROPE_F, dtype=jnp.float32) / ROPE_F)
    ang = pos[:, :, None] * inv
    return jnp.cos(ang), jnp.sin(ang)


def apply_rope(x, cos, sin):
    b, L, H, _ = x.shape
    xr = x.astype(jnp.float32).reshape(b, L, H, 2, 2, ROPE_F)
    x1, x2 = xr[..., 0, :], xr[..., 1, :]
    c = cos[None, :, None]
    s = sin[None, :, None]
    out = jnp.stack([x1 * c - x2 * s, x1 * s + x2 * c], axis=-2)
    return out.reshape(b, L, H, HD).astype(x.dtype)


def block_attention(q, k, v):
    b, L = q.shape[0], q.shape[1]
    nb = L // Q_BLOCK
    rep = H_Q // H_KV
    qb = q.reshape(b, nb, Q_BLOCK, H_KV, rep, HD).transpose(1, 0, 2, 3, 4, 5)
    scale = HD ** -0.5

    def one_block(qi):
        s = jnp.einsum('bqgrd,btgd->bgrqt', qi, k, preferred_element_type=jnp.float32) * scale
        p = jax.nn.softmax(s, axis=-1).astype(v.dtype)
        return jnp.einsum('bgrqt,btgd->bqgrd', p, v)

    o = lax.map(one_block, qb)
    return o.transpose(1, 0, 2, 3, 4, 5).reshape(b, L, Q_DIM)


def centred_depthwise_conv(x, w, bias):
    y = lax.conv_general_dilated(x, w.astype(x.dtype)[:, None, :], window_strides=(1,),
                                 padding=[(CONV_K // 2, CONV_K // 2)],
                                 dimension_numbers=('NWC', 'WIO', 'NWC'),
                                 feature_group_count=x.shape[-1])
    return y + bias.astype(x.dtype)


def ssd_chunked(x, dt, A, Bm, Cm, h0):
    b, L, H, P = x.shape
    G, N = Bm.shape[-2], Bm.shape[-1]
    R = H // G
    nc = L // CHUNK
    x = x.reshape(b, nc, CHUNK, G, R, P)
    dt = dt.reshape(b, nc, CHUNK, G, R)
    Bm = Bm.reshape(b, nc, CHUNK, G, N)
    Cm = Cm.reshape(b, nc, CHUNK, G, N)
    acs = lax.cumsum(dt * A.reshape(G, R), axis=2)
    acs_t = jnp.moveaxis(acs, 2, -1)
    seg = acs_t[..., :, None] - acs_t[..., None, :]
    mask = jnp.tril(jnp.ones((CHUNK, CHUNK), dtype=bool))
    decay = jnp.where(mask, jnp.exp(jnp.where(mask, seg, 0.0)), 0.0)
    cb = jnp.einsum('bcign,bcjgn->bcgij', Cm, Bm)
    y_diag = jnp.einsum('bcgij,bcgrij,bcjgr,bcjgrp->bcigrp', cb, decay, dt, x)
    decay_to_end = jnp.exp(acs[:, :, -1:] - acs)
    states = jnp.einsum('bcjgn,bcjgr,bcjgrp->bcgrpn', Bm, decay_to_end * dt, x)
    chunk_decay = jnp.exp(acs[:, :, -1])

    def step(h, inp):
        s, d = inp
        return d[..., None, None] * h + s, h

    h_final, h_prev = lax.scan(step, h0.reshape(b, G, R, P, N),
                               (jnp.moveaxis(states, 1, 0), jnp.moveaxis(chunk_decay, 1, 0)))
    h_prev = jnp.moveaxis(h_prev, 0, 1)
    y_off = jnp.einsum('bcign,bcigr,bcgrpn->bcigrp', Cm, jnp.exp(acs), h_prev)
    y = (y_diag + y_off).reshape(b, L, H, P)
    return y, h_final.reshape(b, H, P, N)


def mixer(h, p, rope, ctx_k, ctx_v, h0_f, h0_b):
    b, L, _ = h.shape
    proj = h @ p['w_in']
    q, k, v, z, xbc, dt, gates = jnp.split(proj, SPLIT_IDX, axis=-1)
    q = rms_norm(q.reshape(b, L, H_Q, HD), p['q_norm'])
    k = rms_norm(k.reshape(b, L, H_KV, HD), p['k_norm'])
    v = v.reshape(b, L, H_KV, HD)
    if rope is None:
        qr, kr = q, k
    else:
        cos, sin = rope
        qr, kr = apply_rope(q, cos, sin), apply_rope(k, cos, sin)
    if ctx_k is None:
        k_all, v_all = kr, v
    else:
        k_all = jnp.concatenate([ctx_k.astype(kr.dtype), kr], axis=1)
        v_all = jnp.concatenate([ctx_v.astype(v.dtype), v], axis=1)
    attn = block_attention(qr, k_all, v_all)
    xbc = jax.nn.silu(centred_depthwise_conv(xbc, p['conv_w'], p['conv_b']))
    xs, bm, cm = jnp.split(xbc, [D_INNER, D_INNER + BC_DIM], axis=-1)
    xs = xs.reshape(b, L, H_SSD, SSD_P).astype(jnp.float32)
    bm = bm.reshape(b, L, SSD_G, SSD_N).astype(jnp.float32)
    cm = cm.reshape(b, L, SSD_G, SSD_N).astype(jnp.float32)
    dt = dt.astype(jnp.float32)
    dt_bias = p['dt_bias'].astype(jnp.float32)
    a = -jnp.exp(p['a_log'].astype(jnp.float32))
    dt_f = jax.nn.softplus(dt[..., :H_SSD] + dt_bias[0])
    dt_b = jax.nn.softplus(dt[..., H_SSD:] + dt_bias[1])
    y_f, h_f = ssd_chunked(xs, dt_f, a[0], bm, cm, h0_f)
    flip = lambda t: jnp.flip(t, axis=1)
    y_b, h_b = ssd_chunked(flip(xs), flip(dt_b), a[1], flip(bm), flip(cm), h0_b)
    y = y_f + flip(y_b) + p['d_skip'].astype(jnp.float32)[:, None] * xs
    y = y.reshape(b, L, D_INNER).astype(h.dtype)
    y = rms_norm(y * jax.nn.silu(z), p['ssd_norm'])
    g_a, g_b = jnp.split(gates, 2, axis=-1)
    merged = jax.nn.sigmoid(g_a) * (attn @ p['w_attn_o']) + jax.nn.sigmoid(g_b) * (y @ p['w_ssd_o'])
    return merged @ p['w_out'], k, v, h_f, h_b


def trunk_layer(x, mod, p, rope, ctx_k, ctx_v, h0_f, h0_b):
    sh1, sc1, g1, sh2, sc2, g2 = [mod[:, i][:, None, :] for i in range(6)]
    h = layer_norm(x) * (1 + sc1) + sh1
    mix, k, v, h_f, h_b = mixer(h, p, rope, ctx_k, ctx_v, h0_f, h0_b)
    x = layer_norm(ALPHA * x + g1 * mix, p['ln1_g'], p['ln1_b'])
    h = layer_norm(x) * (1 + sc2) + sh2
    ff = jnp.square(jax.nn.relu(h @ p['w_mlp1'] + p['b_mlp1'])) @ p['w_mlp2'] + p['b_mlp2']
    x = layer_norm(ALPHA * x + g2 * ff, p['ln2_g'], p['ln2_b'])
    return x, k, v, h_f, h_b


def setup_inputs(seed: int = 0) -> dict:
    key = jax.random.key(seed)
    ks = jax.random.split(key, 32)
    f32 = jnp.float32
    nrm = lambda k, shape, s: jax.random.normal(k, shape, f32) * s
    dt0 = jnp.exp(jax.random.uniform(ks[11], (DEPTH, 2, H_SSD), f32, math.log(1e-3), math.log(1e-1)))
    return {
        "x_prompt": nrm(ks[0], (BATCH, SEQ, D_MODEL), 1.0),
        "x_sample": nrm(ks[1], (DEC_BATCH, DEC_SEQ, D_MODEL), 1.0),
        "cache_k": nrm(ks[2], (DEC_BATCH, DEPTH, PAST_LEN, H_KV, HD), 1.0),
        "cache_v": nrm(ks[3], (DEC_BATCH, DEPTH, PAST_LEN, H_KV, HD), 1.0),
        "state_ssd": nrm(ks[4], (DEC_BATCH, DEPTH, 2, H_SSD, SSD_P, SSD_N), 0.1),
        "c": nrm(ks[5], (DEC_BATCH, D_MODEL), 1.0),
        "c_ctx": nrm(ks[6], (D_MODEL,), 1.0),
        "w_mod": nrm(ks[7], (DEPTH, D_MODEL, 6 * D_MODEL), 0.5 * D_MODEL ** -0.5),
        "b_mod": nrm(ks[8], (DEPTH, 6 * D_MODEL), 0.01),
        "w_in": nrm(ks[9], (DEPTH, D_MODEL, IN_DIM), D_MODEL ** -0.5),
        "q_norm": 1.0 + nrm(ks[10], (DEPTH, HD), 0.05),
        "k_norm": 1.0 + nrm(ks[12], (DEPTH, HD), 0.05),
        "conv_w": nrm(ks[13], (DEPTH, CONV_K, XBC_DIM), CONV_K ** -0.5),
        "conv_b": nrm(ks[14], (DEPTH, XBC_DIM), 0.01),
        "a_log": jnp.log(jax.random.uniform(ks[15], (DEPTH, 2, H_SSD), f32, 1.0, 16.0)),
        "dt_bias": dt0 + jnp.log(-jnp.expm1(-dt0)),
        "d_skip": 1.0 + nrm(ks[16], (DEPTH, H_SSD), 0.05),
        "ssd_norm": 1.0 + nrm(ks[17], (DEPTH, D_INNER), 0.05),
        "w_attn_o": nrm(ks[18], (DEPTH, Q_DIM, D_MODEL), Q_DIM ** -0.5),
        "w_ssd_o": nrm(ks[19], (DEPTH, D_INNER, D_MODEL), D_INNER ** -0.5),
        "w_out": nrm(ks[20], (DEPTH, D_MODEL, D_MODEL), BETA * D_MODEL ** -0.5),
        "ln1_g": 1.0 + nrm(ks[21], (DEPTH, D_MODEL), 0.05),
        "ln1_b": nrm(ks[22], (DEPTH, D_MODEL), 0.01),
        "w_mlp1": nrm(ks[23], (DEPTH, D_MODEL, D_FF), D_MODEL ** -0.5),
        "b_mlp1": nrm(ks[24], (DEPTH, D_FF), 0.01),
        "w_mlp2": nrm(ks[25], (DEPTH, D_FF, D_MODEL), BETA * D_FF ** -0.5),
        "b_mlp2": nrm(ks[26], (DEPTH, D_MODEL), 0.01),
        "ln2_g": 1.0 + nrm(ks[27], (DEPTH, D_MODEL), 0.05),
        "ln2_b": nrm(ks[28], (DEPTH, D_MODEL), 0.01),
    }


def reference(x_prompt, x_sample, cache_k, cache_v, state_ssd, c, c_ctx, w_mod, b_mod, w_in,
              q_norm, k_norm, conv_w, conv_b, a_log, dt_bias, d_skip, ssd_norm, w_attn_o, w_ssd_o,
              w_out, ln1_g, ln1_b, w_mlp1, b_mlp1, w_mlp2, b_mlp2, ln2_g, ln2_b):
    rope = rope_tables(x_sample.shape[1])
    y_p = x_prompt
    y_s = x_sample
    ks_l, vs_l, ss_l = [], [], []
    for l in range(DEPTH):
        p = dict(w_in=w_in[l], q_norm=q_norm[l], k_norm=k_norm[l], conv_w=conv_w[l], conv_b=conv_b[l],
                 a_log=a_log[l], dt_bias=dt_bias[l], d_skip=d_skip[l], ssd_norm=ssd_norm[l],
                 w_attn_o=w_attn_o[l], w_ssd_o=w_ssd_o[l], w_out=w_out[l], ln1_g=ln1_g[l], ln1_b=ln1_b[l],
                 w_mlp1=w_mlp1[l], b_mlp1=b_mlp1[l], w_mlp2=w_mlp2[l], b_mlp2=b_mlp2[l],
                 ln2_g=ln2_g[l], ln2_b=ln2_b[l])
        mod_ctx = (jax.nn.silu(c_ctx) @ w_mod[l] + b_mod[l]).reshape(1, 6, D_MODEL)
        mod_lat = (jax.nn.silu(c) @ w_mod[l] + b_mod[l]).reshape(c.shape[0], 6, D_MODEL)
        zeros = jnp.zeros((y_p.shape[0], H_SSD, SSD_P, SSD_N), jnp.float32)
        y_p, k_l, v_l, hf, hb = trunk_layer(y_p, mod_ctx, p, None, None, None, zeros, zeros)
        ks_l.append(k_l)
        vs_l.append(v_l)
        ss_l.append(jnp.stack([hf, hb], axis=1).astype(x_prompt.dtype))
        y_s, _, _, _, _ = trunk_layer(y_s, mod_lat, p, rope, cache_k[:, l], cache_v[:, l],
                                      state_ssd[:, l, 0].astype(jnp.float32),
                                      state_ssd[:, l, 1].astype(jnp.float32))
    new_cache_k = jnp.stack(ks_l, axis=1)
    new_cache_v = jnp.stack(vs_l, axis=1)
    new_state_ssd = jnp.stack(ss_l, axis=1)
    return (y_p, y_s, new_cache_k, new_cache_v, new_state_ssd)
```

```python
import functools
import math

import jax
import jax.numpy as jnp
from jax import lax
from jax.experimental import pallas as pl
from jax.experimental.pallas import tpu as pltpu

D_MODEL = 1024
BATCH = 16
SEQ = 256
DEC_BATCH = 4
DEC_SEQ = 1024
PAST_LEN = 512
GRID_W = 64
H_Q = 8
H_KV = 2
HD = 128
ROPE_F = HD // 4
ROPE_THETA = 10000.0
D_INNER = 2 * D_MODEL
SSD_P = 64
H_SSD = D_INNER // SSD_P
SSD_G = 8
SSD_R = H_SSD // SSD_G
SSD_N = 128
CONV_K = 5
CHUNK = 128
D_FF = 4 * D_MODEL
DEPTH = 1
ALPHA = (2.0 * DEPTH) ** 0.25
EPS = 1e-6
Q_DIM = H_Q * HD
KV_DIM = H_KV * HD
BC_DIM = SSD_G * SSD_N
XBC_DIM = D_INNER + 2 * BC_DIM
DT_DIM = 2 * H_SSD
REP = H_Q // H_KV

N_CTX = BATCH * SEQ
N_LAT = DEC_BATCH * DEC_SEQ
N_TOK = N_CTX + N_LAT

LANE = 128

COL_Z = 0
COL_GATE = COL_Z + D_INNER
COL_X = COL_GATE + 2 * D_MODEL
COL_B = COL_X + D_INNER
COL_C = COL_B + BC_DIM
COL_Q = COL_C + BC_DIM
COL_K = COL_Q + Q_DIM
COL_V = COL_K + KV_DIM
COL_DT = COL_V + KV_DIM
PROJ_W = COL_DT + LANE

VMEM_LIMIT = 52 * 1024 * 1024

F32 = jnp.float32
BF16 = jnp.bfloat16


def _ln(x):
    mu = jnp.mean(x, axis=-1, keepdims=True)
    xc = x - mu
    var = jnp.mean(xc * xc, axis=-1, keepdims=True)
    return xc * lax.rsqrt(var + EPS)


def _rms(x, g):
    return x * lax.rsqrt(jnp.mean(x * x, axis=-1, keepdims=True) + EPS) * g


def _mod_row(i, tm):
    nct = N_CTX // tm
    tpl = DEC_SEQ // tm
    return jnp.where(i < nct, 0, 1 + jnp.maximum(i - nct, 0) // tpl)


def _params(sem):
    return pltpu.CompilerParams(dimension_semantics=sem, vmem_limit_bytes=VMEM_LIMIT)


def _mod_kernel(c_ref, w_ref, b_ref, o_ref):
    c = c_ref[...]
    s = c * jax.nn.sigmoid(c)
    o_ref[...] = jnp.dot(s.astype(BF16), w_ref[...].astype(BF16), preferred_element_type=F32) + b_ref[...]


def _modulation(c_all, w_mod, b_mod):
    tn = 1536
    return pl.pallas_call(
        _mod_kernel,
        grid=(6 * D_MODEL // tn,),
        in_specs=[pl.BlockSpec((8, D_MODEL), lambda j: (0, 0)),
                  pl.BlockSpec((D_MODEL, tn), lambda j: (0, j)),
                  pl.BlockSpec((1, tn), lambda j: (0, j))],
        out_specs=pl.BlockSpec((8, tn), lambda j: (0, j)),
        out_shape=jax.ShapeDtypeStruct((8, 6 * D_MODEL), F32),
        compiler_params=_params(("arbitrary",)),
        name="modulation",
    )(c_all, w_mod, b_mod)


def _inproj_kernel(x_ref, mod_ref, w_ref, o_ref, h_scr):
    @pl.when(pl.program_id(1) == 0)
    def _():
        h = _ln(x_ref[...]) * (1.0 + mod_ref[1:2, :]) + mod_ref[0:1, :]
        h_scr[...] = h.astype(BF16)

    o_ref[...] = jnp.dot(h_scr[...], w_ref[...], preferred_element_type=F32)


def _inproj(x_all, mod, w_in_b):
    tm, tn = 512, 1408
    return pl.pallas_call(
        _inproj_kernel,
        grid=(N_TOK // tm, PROJ_W // tn),
        in_specs=[pl.BlockSpec((tm, D_MODEL), lambda i, j: (i, 0)),
                  pl.BlockSpec((None, 6, D_MODEL), lambda i, j: (_mod_row(i, tm), 0, 0)),
                  pl.BlockSpec((D_MODEL, tn), lambda i, j: (0, j))],
        out_specs=pl.BlockSpec((tm, tn), lambda i, j: (i, j)),
        out_shape=jax.ShapeDtypeStruct((N_TOK, PROJ_W), F32),
        scratch_shapes=[pltpu.VMEM((tm, D_MODEL), BF16)],
        compiler_params=_params(("arbitrary", "arbitrary")),
        name="inproj",
    )(x_all, mod, w_in_b)


def _dtprep_kernel(p_ref, bias_ref, alog_ref, dt_ref, acs_ref, *, nchunk):
    a_head = -jnp.exp(alog_ref[...])
    ii = lax.broadcasted_iota(jnp.int32, (CHUNK, CHUNK), 0)
    jj = lax.broadcasted_iota(jnp.int32, (CHUNK, CHUNK), 1)
    upper = (ii <= jj).astype(F32)
    lower = (ii >= jj).astype(F32)
    is_fwd = (ii & 7) < SSD_R
    for c in range(nchunk):
        t = p_ref[c * CHUNK:(c + 1) * CHUNK, :]
        tt = t.T + bias_ref[...]
        dtv = jnp.maximum(tt, 0.0) + jnp.log1p(jnp.exp(-jnp.abs(tt)))
        a = dtv * a_head
        pre = jnp.dot(a, upper, precision=lax.Precision.HIGHEST, preferred_element_type=F32)
        suf = jnp.dot(a, lower, precision=lax.Precision.HIGHEST, preferred_element_type=F32)
        dt_ref[c] = dtv
        acs_ref[c] = jnp.where(is_fwd, pre, suf)


def _dtprep(proj, bias_col, alog_col):
    nchunk = 4
    tm = nchunk * CHUNK
    nct = N_TOK // CHUNK
    out = jax.ShapeDtypeStruct((nct, LANE, CHUNK), F32)
    return pl.pallas_call(
        functools.partial(_dtprep_kernel, nchunk=nchunk),
        grid=(N_TOK // tm,),
        in_specs=[pl.BlockSpec((tm, LANE), lambda i: (i, COL_DT // LANE)),
                  pl.BlockSpec((LANE, 1), lambda i: (0, 0)),
                  pl.BlockSpec((LANE, 1), lambda i: (0, 0))],
        out_specs=[pl.BlockSpec((nchunk, LANE, CHUNK), lambda i: (i, 0, 0)),
                   pl.BlockSpec((nchunk, LANE, CHUNK), lambda i: (i, 0, 0))],
        out_shape=[out, out],
        compiler_params=_params(("arbitrary",)),
        name="dtprep",
    )(proj, bias_col, alog_col)


def _rope(x, cos, sa, sb):
    return x * cos + pltpu.roll(x, HD - ROPE_F, axis=1) * sa + pltpu.roll(x, ROPE_F, axis=1) * sb


def _softmax_pv(q4, k_b, v_b):
    s = lax.dot_general(q4.astype(BF16), k_b, (((1,), (1,)), ((), ())), preferred_element_type=F32)
    s = s * (HD ** -0.5)
    m = jnp.max(s, axis=-1, keepdims=True)
    p = jnp.exp(s - m)
    l = jnp.sum(p, axis=-1, keepdims=True)
    o = jnp.dot(p.astype(BF16), v_b, preferred_element_type=F32)
    return o / l


def _attn_ctx_kernel(q_ref, k_ref, v_ref, qn_ref, kn_ref, o_ref, ko_ref, vo_ref):
    k = _rms(k_ref[...], kn_ref[...])
    v = v_ref[...]
    ko_ref[...] = k
    vo_ref[...] = v
    q = q_ref[...]
    q4 = jnp.concatenate([_rms(q[:, h * HD:(h + 1) * HD], qn_ref[...]) for h in range(REP)], axis=0)
    o = _softmax_pv(q4, k.astype(BF16), v.astype(BF16))
    for h in range(REP):
        o_ref[:, h * HD:(h + 1) * HD] = o[h * SEQ:(h + 1) * SEQ].astype(BF16)


def _attn_ctx(proj, q_norm, k_norm):
    qw = REP * HD
    kv_out = jax.ShapeDtypeStruct((BATCH, SEQ, KV_DIM), F32)
    return pl.pallas_call(
        _attn_ctx_kernel,
        grid=(BATCH, H_KV),
        in_specs=[pl.BlockSpec((SEQ, qw), lambda b, g: (b, COL_Q // qw + g)),
                  pl.BlockSpec((SEQ, HD), lambda b, g: (b, COL_K // HD + g)),
                  pl.BlockSpec((SEQ, HD), lambda b, g: (b, COL_V // HD + g)),
                  pl.BlockSpec((1, HD), lambda b, g: (0, 0)),
                  pl.BlockSpec((1, HD), lambda b, g: (0, 0))],
        out_specs=[pl.BlockSpec((SEQ, qw), lambda b, g: (b, g)),
                   pl.BlockSpec((None, SEQ, HD), lambda b, g: (b, 0, g)),
                   pl.BlockSpec((None, SEQ, HD), lambda b, g: (b, 0, g))],
        out_shape=[jax.ShapeDtypeStruct((N_CTX, Q_DIM), BF16), kv_out, kv_out],
        compiler_params=_params(("arbitrary", "arbitrary")),
        name="attn_ctx",
    )(proj, proj, proj, q_norm, k_norm)


def _attn_lat_kernel(q_ref, k_ref, v_ref, ck_ref, cv_ref, cos_ref, sa_ref, sb_ref, qn_ref, kn_ref,
                     o_ref, kall, vall, *, tq):
    qi = pl.program_id(2)

    @pl.when(qi == 0)
    def _():
        k = _rope(_rms(k_ref[...], kn_ref[...]), cos_ref[...], sa_ref[...], sb_ref[...])
        kall[0:PAST_LEN, :] = ck_ref[...].astype(BF16)
        kall[PAST_LEN:, :] = k.astype(BF16)
        vall[0:PAST_LEN, :] = cv_ref[...].astype(BF16)
        vall[PAST_LEN:, :] = v_ref[...].astype(BF16)

    r0 = pl.multiple_of(qi * tq, tq)
    cos = cos_ref[pl.ds(r0, tq), :]
    sa = sa_ref[pl.ds(r0, tq), :]
    sb = sb_ref[pl.ds(r0, tq), :]
    q = q_ref[...]
    q4 = jnp.concatenate(
        [_rope(_rms(q[:, h * HD:(h + 1) * HD], qn_ref[...]), cos, sa, sb) for h in range(REP)], axis=0)
    o = _softmax_pv(q4, kall[...], vall[...])
    for h in range(REP):
        o_ref[:, h * HD:(h + 1) * HD] = o[h * tq:(h + 1) * tq].astype(BF16)


def _attn_lat(proj, cache_k, cache_v, cos, sa, sb, q_norm, k_norm):
    tq = 256
    qw = REP * HD
    nq = DEC_SEQ // tq
    row0 = N_CTX // tq
    tab = pl.BlockSpec((DEC_SEQ, HD), lambda b, g, i: (0, 0))
    return pl.pallas_call(
        functools.partial(_attn_lat_kernel, tq=tq),
        grid=(DEC_BATCH, H_KV, nq),
        in_specs=[pl.BlockSpec((tq, qw), lambda b, g, i: (row0 + b * nq + i, COL_Q // qw + g)),
                  pl.BlockSpec((DEC_SEQ, HD), lambda b, g, i: (N_CTX // DEC_SEQ + b, COL_K // HD + g)),
                  pl.BlockSpec((DEC_SEQ, HD), lambda b, g, i: (N_CTX // DEC_SEQ + b, COL_V // HD + g)),
                  pl.BlockSpec((None, PAST_LEN, HD), lambda b, g, i: (b, 0, g)),
                  pl.BlockSpec((None, PAST_LEN, HD), lambda b, g, i: (b, 0, g)),
                  tab, tab, tab,
                  pl.BlockSpec((1, HD), lambda b, g, i: (0, 0)),
                  pl.BlockSpec((1, HD), lambda b, g, i: (0, 0))],
        out_specs=pl.BlockSpec((tq, qw), lambda b, g, i: (b * nq + i, g)),
        out_shape=jax.ShapeDtypeStruct((N_LAT, Q_DIM), BF16),
        scratch_shapes=[pltpu.VMEM((PAST_LEN + DEC_SEQ, HD), BF16),
                        pltpu.VMEM((PAST_LEN + DEC_SEQ, HD), BF16)],
        compiler_params=_params(("arbitrary", "arbitrary", "arbitrary")),
        name="attn_lat",
    )(proj, proj, proj, cache_k, cache_v, cos, sa, sb, q_norm, k_norm)


def _conv_silu(v, w_ref, b_ref):
    n = v.shape[0]
    rows = lax.broadcasted_iota(jnp.int32, v.shape, 0)
    acc = v * w_ref[CONV_K // 2:CONV_K // 2 + 1, :]
    for k in range(CONV_K):
        d = k - CONV_K // 2
        if d == 0:
            continue
        shifted = pltpu.roll(v, (-d) % n, axis=0)
        valid = (rows + d >= 0) & (rows + d < n)
        acc = acc + jnp.where(valid, shifted, 0.0) * w_ref[k:k + 1, :]
    acc = acc + b_ref[...]
    return acc * jax.nn.sigmoid(acc)


def _ssd_kernel(*refs, seq, has_h0, emit_state):
    (x_ref, b_ref, c_ref, dt_ref, acs_ref, cwx_ref, cwb_ref, cwc_ref,
     cbx_ref, cbb_ref, cbc_ref, dsk_ref) = refs[:12]
    pos = 12
    h0_ref = None
    if has_h0:
        h0_ref = refs[pos]
        pos += 1
    y_ref = refs[pos]
    pos += 1
    hout_ref = None
    if emit_state:
        hout_ref = refs[pos]
        pos += 1
    xs, bs, cs, dtc, acc, ht = refs[pos:]
    nc = seq // CHUNK
    half = SSD_R * SSD_P // 2

    xs[...] = _conv_silu(x_ref[...], cwx_ref, cbx_ref)
    bs[...] = _conv_silu(b_ref[...], cwb_ref, cbb_ref)
    cs[...] = _conv_silu(c_ref[...], cwc_ref, cbc_ref)
    y_ref[...] = xs[...] * dsk_ref[...]

    for c in range(nc):
        dtc[c * CHUNK:(c + 1) * CHUNK, :] = jnp.concatenate([dt_ref[c]] * (CHUNK // 8), axis=0).T
        acc[c * CHUNK:(c + 1) * CHUNK, :] = jnp.concatenate([acs_ref[c]] * (CHUNK // 8), axis=0).T

    for d in range(2):
        if has_h0:
            ht[d, :, 0:half] = h0_ref[d, 0:half, :].T
            ht[d, :, half:] = h0_ref[d, half:, :].T
        else:
            ht[d] = jnp.zeros((SSD_N, 2 * half), F32)

    ii = lax.broadcasted_iota(jnp.int32, (CHUNK, CHUNK), 0)
    jj = lax.broadcasted_iota(jnp.int32, (CHUNK, CHUNK), 1)
    lane_lo = jj < SSD_P

    def pair(a, b):
        return jnp.where(lane_lo, a, b)

    def bcast(col):
        return jnp.broadcast_to(col, (CHUNK, CHUNK))

    def chunk(c, d):
        r0 = pl.multiple_of(c * CHUNK, CHUNK)
        xc = xs[pl.ds(r0, CHUNK), :]
        bt = bs[pl.ds(r0, CHUNK), :].T.astype(BF16)
        cc = cs[pl.ds(r0, CHUNK), :].astype(BF16)
        cbm = jnp.dot(cc, bt, preferred_element_type=F32)
        dcol = dtc[pl.ds(r0, CHUNK), :]
        acol = acc[pl.ds(r0, CHUNK), :]
        arow = acs_ref[c]
        h_prev = ht[d]
        yoff = jnp.dot(cc, h_prev.astype(BF16), preferred_element_type=F32)
        mask = (ii >= jj) if d == 0 else (ii <= jj)
        last = CHUNK - 1 if d == 0 else 0
        ms, dts, wts, eas, cds = [], [], [], [], []
        for r in range(SSD_R):
            col = d * SSD_R + r
            ac = acol[:, col:col + 1]
            ar = arow[col:col + 1, :]
            seg = ac - ar
            dec = jnp.where(mask, jnp.exp(jnp.where(mask, seg, 0.0)), 0.0)
            ms.append((cbm * dec).astype(BF16))
            dt_c = dcol[:, col:col + 1]
            a_last = ar[:, last:last + 1]
            dts.append(bcast(dt_c))
            wts.append(bcast(dt_c * jnp.exp(a_last - ac)))
            eas.append(bcast(jnp.exp(ac)))
            cds.append(bcast(jnp.exp(a_last)))
        ys, sts, cdm = [], [], []
        for p in range(2):
            xp = xc[:, p * LANE:(p + 1) * LANE]
            r_a, r_b = 2 * p, 2 * p + 1
            xdt = (xp * pair(dts[r_a], dts[r_b])).astype(BF16)
            yd = pair(jnp.dot(ms[r_a], xdt, preferred_element_type=F32),
                      jnp.dot(ms[r_b], xdt, preferred_element_type=F32))
            ys.append(yd + yoff[:, p * LANE:(p + 1) * LANE] * pair(eas[r_a], eas[r_b]))
            sts.append((xp * pair(wts[r_a], wts[r_b])).astype(BF16))
            cdm.append(pair(cds[r_a], cds[r_b]))
        st = jnp.dot(bt, jnp.concatenate(sts, axis=1), preferred_element_type=F32)
        ht[d] = h_prev * jnp.concatenate(cdm, axis=1) + st
        y_ref[pl.ds(r0, CHUNK), :] += jnp.concatenate(ys, axis=1)

    def body(t, carry):
        chunk(t, 0)
        chunk(nc - 1 - t, 1)
        return carry

    lax.fori_loop(0, nc, body, 0)

    if emit_state:
        for d in range(2):
            hout_ref[d, 0:half, :] = ht[d, :, 0:half].T
            hout_ref[d, half:, :] = ht[d, :, half:].T


def _ssd(proj, dt_t, acs_t, conv_w, conv_b, dskip, h0, *, seq, nbatch, row_blk0, has_h0, emit_state):
    gw = SSD_R * SSD_P
    nc = seq // CHUNK
    xb0, bb0, cb0 = COL_X // gw, COL_B // SSD_N, COL_C // SSD_N
    in_specs = [
        pl.BlockSpec((seq, gw), lambda b, g: (row_blk0 + b, xb0 + g)),
        pl.BlockSpec((seq, SSD_N), lambda b, g: (row_blk0 + b, bb0 + g)),
        pl.BlockSpec((seq, SSD_N), lambda b, g: (row_blk0 + b, cb0 + g)),
        pl.BlockSpec((nc, 8, CHUNK), lambda b, g: (row_blk0 + b, g, 0)),
        pl.BlockSpec((nc, 8, CHUNK), lambda b, g: (row_blk0 + b, g, 0)),
        pl.BlockSpec((CONV_K, gw), lambda b, g: (0, g)),
        pl.BlockSpec((CONV_K, SSD_N), lambda b, g: (0, D_INNER // SSD_N + g)),
        pl.BlockSpec((CONV_K, SSD_N), lambda b, g: (0, (D_INNER + BC_DIM) // SSD_N + g)),
        pl.BlockSpec((1, gw), lambda b, g: (0, g)),
        pl.BlockSpec((1, SSD_N), lambda b, g: (0, D_INNER // SSD_N + g)),
        pl.BlockSpec((1, SSD_N), lambda b, g: (0, (D_INNER + BC_DIM) // SSD_N + g)),
        pl.BlockSpec((1, gw), lambda b, g: (0, g)),
    ]
    args = [proj, proj, proj, dt_t, acs_t, conv_w, conv_w, conv_w, conv_b, conv_b, conv_b, dskip]
    state_spec = pl.BlockSpec((None, 2, gw, SSD_N), lambda b, g: (b, 0, g, 0))
    if has_h0:
        in_specs.append(state_spec)
        args.append(h0)
    out_specs = [pl.BlockSpec((seq, gw), lambda b, g: (b, g))]
    out_shape = [jax.ShapeDtypeStruct((nbatch * seq, D_INNER), F32)]
    if emit_state:
        out_specs.append(state_spec)
        out_shape.append(jax.ShapeDtypeStruct((nbatch, 2, H_SSD * SSD_P, SSD_N), F32))
    return pl.pallas_call(
        functools.partial(_ssd_kernel, seq=seq, has_h0=has_h0, emit_state=emit_state),
        grid=(nbatch, SSD_G),
        in_specs=in_specs,
        out_specs=out_specs,
        out_shape=out_shape,
        scratch_shapes=[pltpu.VMEM((seq, gw), F32), pltpu.VMEM((seq, SSD_N), F32),
                        pltpu.VMEM((seq, SSD_N), F32), pltpu.VMEM((seq, LANE), F32),
                        pltpu.VMEM((seq, LANE), F32), pltpu.VMEM((2, SSD_N, gw), F32)],
        compiler_params=_params(("arbitrary", "arbitrary")),
        name="ssd_lat" if has_h0 else "ssd_ctx",
    )(*args)


def _merge_kernel(attn_ref, y_ref, z_ref, g_ref, x_ref, mod_ref, nrm_ref, wa_ref, ws_ref, wo_ref,
                  l1g_ref, l1b_ref, o_ref):
    z = z_ref[...]
    yg = y_ref[...] * (z * jax.nn.sigmoid(z))
    yn = yg * lax.rsqrt(jnp.mean(yg * yg, axis=-1, keepdims=True) + EPS) * nrm_ref[...]
    pa = jnp.dot(attn_ref[...], wa_ref[...], preferred_element_type=F32)
    ps = jnp.dot(yn.astype(BF16), ws_ref[...], preferred_element_type=F32)
    g = g_ref[...]
    merged = jax.nn.sigmoid(g[:, :D_MODEL]) * pa + jax.nn.sigmoid(g[:, D_MODEL:]) * ps
    mix = jnp.dot(merged.astype(BF16), wo_ref[...], preferred_element_type=F32)
    x1 = _ln(ALPHA * x_ref[...] + mod_ref[2:3, :] * mix)
    o_ref[...] = x1 * l1g_ref[...] + l1b_ref[...]


def _merge(attn, yssd, proj, x_all, mod, ssd_norm, wa, ws, wo, l1g, l1b):
    tm = 256
    const = lambda shape: pl.BlockSpec(shape, lambda i: (0, 0))
    return pl.pallas_call(
        _merge_kernel,
        grid=(N_TOK // tm,),
        in_specs=[pl.BlockSpec((tm, Q_DIM), lambda i: (i, 0)),
                  pl.BlockSpec((tm, D_INNER), lambda i: (i, 0)),
                  pl.BlockSpec((tm, D_INNER), lambda i: (i, COL_Z // D_INNER)),
                  pl.BlockSpec((tm, 2 * D_MODEL), lambda i: (i, COL_GATE // (2 * D_MODEL))),
                  pl.BlockSpec((tm, D_MODEL), lambda i: (i, 0)),
                  pl.BlockSpec((None, 6, D_MODEL), lambda i: (_mod_row(i, tm), 0, 0)),
                  const((1, D_INNER)), const((Q_DIM, D_MODEL)), const((D_INNER, D_MODEL)),
                  const((D_MODEL, D_MODEL)), const((1, D_MODEL)), const((1, D_MODEL))],
        out_specs=pl.BlockSpec((tm, D_MODEL), lambda i: (i, 0)),
        out_shape=jax.ShapeDtypeStruct((N_TOK, D_MODEL), F32),
        compiler_params=_params(("arbitrary",)),
        name="merge",
    )(attn, yssd, proj, proj, x_all, mod, ssd_norm, wa, ws, wo, l1g, l1b)


def _mlp_kernel(x1_ref, mod_ref, w1_ref, b1_ref, w2_ref, b2_ref, l2g_ref, l2b_ref, o_ref, h_scr, acc_scr):
    j = pl.program_id(1)

    @pl.when(j == 0)
    def _():
        h = _ln(x1_ref[...]) * (1.0 + mod_ref[4:5, :]) + mod_ref[3:4, :]
        h_scr[...] = h.astype(BF16)
        acc_scr[...] = jnp.zeros_like(acc_scr)

    u = jnp.dot(h_scr[...], w1_ref[...], preferred_element_type=F32) + b1_ref[...]
    u = jnp.maximum(u, 0.0)
    acc_scr[...] += jnp.dot((u * u).astype(BF16), w2_ref[...], preferred_element_type=F32)

    @pl.when(j == pl.num_programs(1) - 1)
    def _():
        ff = acc_scr[...] + b2_ref[...]
        x2 = _ln(ALPHA * x1_ref[...] + mod_ref[5:6, :] * ff)
        o_ref[...] = x2 * l2g_ref[...] + l2b_ref[...]


def _mlp(x1, mod, w1, b1, w2, b2, l2g, l2b):
    tm, tf = 512, 1024
    return pl.pallas_call(
        _mlp_kernel,
        grid=(N_TOK // tm, D_FF // tf),
        in_specs=[pl.BlockSpec((tm, D_MODEL), lambda i, j: (i, 0)),
                  pl.BlockSpec((None, 6, D_MODEL), lambda i, j: (_mod_row(i, tm), 0, 0)),
                  pl.BlockSpec((D_MODEL, tf), lambda i, j: (0, j)),
                  pl.BlockSpec((1, tf), lambda i, j: (0, j)),
                  pl.BlockSpec((tf, D_MODEL), lambda i, j: (j, 0)),
                  pl.BlockSpec((1, D_MODEL), lambda i, j: (0, 0)),
                  pl.BlockSpec((1, D_MODEL), lambda i, j: (0, 0)),
                  pl.BlockSpec((1, D_MODEL), lambda i, j: (0, 0))],
        out_specs=pl.BlockSpec((tm, D_MODEL), lambda i, j: (i, 0)),
        out_shape=jax.ShapeDtypeStruct((N_TOK, D_MODEL), F32),
        scratch_shapes=[pltpu.VMEM((tm, D_MODEL), BF16), pltpu.VMEM((tm, D_MODEL), F32)],
        compiler_params=_params(("arbitrary", "arbitrary")),
        name="mlp",
    )(x1, mod, w1, b1, w2, b2, l2g, l2b)


def _dt_perm():
    idx = []
    for n in range(DT_DIM):
        g, d, r = n // 8, (n % 8) // SSD_R, n % SSD_R
        idx.append(d * H_SSD + g * SSD_R + r)
    return jnp.asarray(idx, dtype=jnp.int32)


def _rope_tables():
    t = jnp.arange(DEC_SEQ)
    posn = jnp.stack([t // GRID_W, t % GRID_W], axis=-1).astype(F32)
    inv = ROPE_THETA ** (-jnp.arange(ROPE_F, dtype=F32) / ROPE_F)
    ang = posn[:, :, None] * inv
    cos, sin = jnp.cos(ang), jnp.sin(ang)
    zero = jnp.zeros_like(sin[:, 0])
    cos_t = jnp.concatenate([cos[:, 0], cos[:, 0], cos[:, 1], cos[:, 1]], axis=-1)
    sa_t = jnp.concatenate([-sin[:, 0], zero, -sin[:, 1], zero], axis=-1)
    sb_t = jnp.concatenate([zero, sin[:, 0], zero, sin[:, 1]], axis=-1)
    return cos_t, sa_t, sb_t


def kernel(x_prompt, x_sample, cache_k, cache_v, state_ssd, c, c_ctx, w_mod, b_mod, w_in, q_norm, k_norm, conv_w, conv_b, a_log, dt_bias, d_skip, ssd_norm, w_attn_o, w_ssd_o, w_out, ln1_g, ln1_b, w_mlp1, b_mlp1, w_mlp2, b_mlp2, ln2_g, ln2_b):
    l = 0
    perm = _dt_perm()
    w = w_in[l]
    o_q, o_k, o_v, o_z = 0, Q_DIM, Q_DIM + KV_DIM, Q_DIM + 2 * KV_DIM
    o_xbc = o_z + D_INNER
    o_dt = o_xbc + XBC_DIM
    o_gate = o_dt + DT_DIM
    w_dt = jnp.take(w[:, o_dt:o_dt + DT_DIM], perm, axis=1)
    w_in_b = jnp.concatenate(
        [w[:, o_z:o_z + D_INNER], w[:, o_gate:o_gate + 2 * D_MODEL], w[:, o_xbc:o_xbc + XBC_DIM],
         w[:, o_q:o_q + Q_DIM], w[:, o_k:o_k + KV_DIM], w[:, o_v:o_v + KV_DIM], w_dt,
         jnp.zeros((D_MODEL, LANE - DT_DIM), F32)], axis=1).astype(BF16)
    pad = jnp.zeros((LANE - DT_DIM,), F32)
    bias_col = jnp.concatenate([jnp.take(dt_bias[l].reshape(-1), perm), pad]).reshape(LANE, 1)
    alog_col = jnp.concatenate([jnp.take(a_log[l].reshape(-1), perm), pad]).reshape(LANE, 1)
    dskip_row = jnp.repeat(d_skip[l], SSD_P).reshape(1, D_INNER)
    row = lambda v: v.reshape(1, -1)

    c_all = jnp.concatenate([c_ctx[None, :], c, jnp.zeros((8 - 1 - DEC_BATCH, D_MODEL), F32)], axis=0)
    mod = _modulation(c_all, w_mod[l], row(b_mod[l])).reshape(8, 6, D_MODEL)

    x_all = jnp.concatenate([x_prompt.reshape(N_CTX, D_MODEL), x_sample.reshape(N_LAT, D_MODEL)], axis=0)
    proj = _inproj(x_all, mod, w_in_b)
    dt_t, acs_t = _dtprep(proj, bias_col, alog_col)

    qn, kn = row(q_norm[l]), row(k_norm[l])
    attn_ctx, k_new, v_new = _attn_ctx(proj, qn, kn)
    cos_t, sa_t, sb_t = _rope_tables()
    attn_lat = _attn_lat(proj, cache_k[:, l].reshape(DEC_BATCH, PAST_LEN, KV_DIM),
                         cache_v[:, l].reshape(DEC_BATCH, PAST_LEN, KV_DIM), cos_t, sa_t, sb_t, qn, kn)

    cw, cb = conv_w[l], row(conv_b[l])
    y_ctx, h_new = _ssd(proj, dt_t, acs_t, cw, cb, dskip_row, None, seq=SEQ, nbatch=BATCH,
                        row_blk0=0, has_h0=False, emit_state=True)
    h0 = state_ssd[:, l].reshape(DEC_BATCH, 2, H_SSD * SSD_P, SSD_N)
    (y_lat,) = _ssd(proj, dt_t, acs_t, cw, cb, dskip_row, h0, seq=DEC_SEQ, nbatch=DEC_BATCH,
                    row_blk0=N_CTX // DEC_SEQ, has_h0=True, emit_state=False)

    attn = jnp.concatenate([attn_ctx, attn_lat], axis=0)
    yssd = jnp.concatenate([y_ctx, y_lat], axis=0)
    x1 = _merge(attn, yssd, proj, x_all, mod, row(ssd_norm[l]), w_attn_o[l].astype(BF16),
                w_ssd_o[l].astype(BF16), w_out[l].astype(BF16), row(ln1_g[l]), row(ln1_b[l]))
    out = _mlp(x1, mod, w_mlp1[l].astype(BF16), row(b_mlp1[l]), w_mlp2[l].astype(BF16), row(b_mlp2[l]),
               row(ln2_g[l]), row(ln2_b[l]))

    y_p = out[:N_CTX].reshape(BATCH, SEQ, D_MODEL)
    y_s = out[N_CTX:].reshape(DEC_BATCH, DEC_SEQ, D_MODEL)
    new_cache_k = k_new.reshape(BATCH, DEPTH, SEQ, H_KV, HD)
    new_cache_v = v_new.reshape(BATCH, DEPTH, SEQ, H_KV, HD)
    new_state = h_new.reshape(BATCH, DEPTH, 2, H_SSD, SSD_P, SSD_N)
    return (y_p, y_s, new_cache_k, new_cache_v, new_state)
```

```python
import functools

import jax
import jax.numpy as jnp
from jax import lax
from jax.experimental import pallas as pl
from jax.experimental.pallas import tpu as pltpu

D_MODEL = 1024
BATCH = 16
SEQ = 256
DEC_BATCH = 4
DEC_SEQ = 1024
PAST_LEN = 512
GRID_W = 64
H_Q = 8
H_KV = 2
HD = 128
ROPE_F = HD // 4
ROPE_THETA = 10000.0
D_INNER = 2 * D_MODEL
SSD_P = 64
H_SSD = D_INNER // SSD_P
SSD_G = 8
SSD_R = H_SSD // SSD_G
SSD_N = 128
CONV_K = 5
CHUNK = 128
D_FF = 4 * D_MODEL
DEPTH = 1
ALPHA = (2.0 * DEPTH) ** 0.25
EPS = 1e-6
Q_DIM = H_Q * HD
KV_DIM = H_KV * HD
BC_DIM = SSD_G * SSD_N
XBC_DIM = D_INNER + 2 * BC_DIM
DT_DIM = 2 * H_SSD
REP = H_Q // H_KV

N_CTX = BATCH * SEQ
N_LAT = DEC_BATCH * DEC_SEQ
N_TOK = N_CTX + N_LAT

LANE = 128

COL_Z = 0
COL_GATE = COL_Z + D_INNER
COL_X = COL_GATE + 2 * D_MODEL
COL_B = COL_X + D_INNER
COL_C = COL_B + BC_DIM
COL_Q = COL_C + BC_DIM
COL_K = COL_Q + Q_DIM
COL_V = COL_K + KV_DIM
COL_DT = COL_V + KV_DIM
PROJ_W = COL_DT + LANE

VMEM_LIMIT = 52 * 1024 * 1024

F32 = jnp.float32
BF16 = jnp.bfloat16


def _ln(x):
    mu = jnp.mean(x, axis=-1, keepdims=True)
    xc = x - mu
    var = jnp.mean(xc * xc, axis=-1, keepdims=True)
    return xc * lax.rsqrt(var + EPS)


def _rms(x, g):
    return x * lax.rsqrt(jnp.mean(x * x, axis=-1, keepdims=True) + EPS) * g


def _mod_row(i, tm):
    nct = N_CTX // tm
    tpl = DEC_SEQ // tm
    return jnp.where(i < nct, 0, 1 + jnp.maximum(i - nct, 0) // tpl)


def _ctx_blk(i, tm):
    return jnp.minimum(i, N_CTX // tm - 1)


def _lat_blk(i, tm):
    return jnp.maximum(i - N_CTX // tm, 0)


def _params(sem):
    return pltpu.CompilerParams(dimension_semantics=sem, vmem_limit_bytes=VMEM_LIMIT)


def _mod_kernel(c_ref, w_ref, b_ref, o_ref):
    c = c_ref[...]
    s = c * jax.nn.sigmoid(c)
    o_ref[...] = jnp.dot(s.astype(BF16), w_ref[...].astype(BF16), preferred_element_type=F32) + b_ref[...]


def _modulation(c_all, w_mod, b_mod):
    tn = 1536
    return pl.pallas_call(
        _mod_kernel,
        grid=(6 * D_MODEL // tn,),
        in_specs=[pl.BlockSpec((8, D_MODEL), lambda j: (0, 0)),
                  pl.BlockSpec((D_MODEL, tn), lambda j: (0, j)),
                  pl.BlockSpec((1, tn), lambda j: (0, j))],
        out_specs=pl.BlockSpec((8, tn), lambda j: (0, j)),
        out_shape=jax.ShapeDtypeStruct((8, 6 * D_MODEL), F32),
        compiler_params=_params(("arbitrary",)),
        name="modulation",
    )(c_all, w_mod, b_mod)


def _inproj_kernel(xp_ref, xl_ref, mod_ref, w_ref, o_ref, h_scr, *, tm):
    @pl.when(pl.program_id(1) == 0)
    def _():
        x = jnp.where(pl.program_id(0) < N_CTX // tm, xp_ref[...], xl_ref[...])
        h = _ln(x) * (1.0 + mod_ref[1:2, :]) + mod_ref[0:1, :]
        h_scr[...] = h.astype(BF16)

    o_ref[...] = jnp.dot(h_scr[...], w_ref[...], preferred_element_type=F32)


def _inproj(xp, xl, mod, w_in_b):
    tm, tn = 512, 1408
    return pl.pallas_call(
        functools.partial(_inproj_kernel, tm=tm),
        grid=(N_TOK // tm, PROJ_W // tn),
        in_specs=[pl.BlockSpec((tm, D_MODEL), lambda i, j: (_ctx_blk(i, tm), 0)),
                  pl.BlockSpec((tm, D_MODEL), lambda i, j: (_lat_blk(i, tm), 0)),
                  pl.BlockSpec((None, 6, D_MODEL), lambda i, j: (_mod_row(i, tm), 0, 0)),
                  pl.BlockSpec((D_MODEL, tn), lambda i, j: (0, j))],
        out_specs=pl.BlockSpec((tm, tn), lambda i, j: (i, j)),
        out_shape=jax.ShapeDtypeStruct((N_TOK, PROJ_W), F32),
        scratch_shapes=[pltpu.VMEM((tm, D_MODEL), BF16)],
        compiler_params=_params(("arbitrary", "arbitrary")),
        name="inproj",
    )(xp, xl, mod, w_in_b)


def _dtprep_kernel(p_ref, bias_ref, alog_ref, dt_ref, acs_ref, *, nchunk):
    a_head = -jnp.exp(alog_ref[...])
    ii = lax.broadcasted_iota(jnp.int32, (CHUNK, CHUNK), 0)
    jj = lax.broadcasted_iota(jnp.int32, (CHUNK, CHUNK), 1)
    upper = (ii <= jj).astype(F32)
    lower = (ii >= jj).astype(F32)
    is_fwd = (ii & 7) < SSD_R
    for c in range(nchunk):
        t = p_ref[c * CHUNK:(c + 1) * CHUNK, :]
        tt = t.T + bias_ref[...]
        dtv = jnp.maximum(tt, 0.0) + jnp.log1p(jnp.exp(-jnp.abs(tt)))
        a = dtv * a_head
        pre = jnp.dot(a, upper, precision=lax.Precision.HIGHEST, preferred_element_type=F32)
        suf = jnp.dot(a, lower, precision=lax.Precision.HIGHEST, preferred_element_type=F32)
        dt_ref[c] = dtv
        acs_ref[c] = jnp.where(is_fwd, pre, suf)


def _dtprep(proj, bias_col, alog_col):
    nchunk = 4
    tm = nchunk * CHUNK
    nct = N_TOK // CHUNK
    out = jax.ShapeDtypeStruct((nct, LANE, CHUNK), F32)
    return pl.pallas_call(
        functools.partial(_dtprep_kernel, nchunk=nchunk),
        grid=(N_TOK // tm,),
        in_specs=[pl.BlockSpec((tm, LANE), lambda i: (i, COL_DT // LANE)),
                  pl.BlockSpec((LANE, 1), lambda i: (0, 0)),
                  pl.BlockSpec((LANE, 1), lambda i: (0, 0))],
        out_specs=[pl.BlockSpec((nchunk, LANE, CHUNK), lambda i: (i, 0, 0)),
                   pl.BlockSpec((nchunk, LANE, CHUNK), lambda i: (i, 0, 0))],
        out_shape=[out, out],
        compiler_params=_params(("arbitrary",)),
        name="dtprep",
    )(proj, bias_col, alog_col)


def _rope(x, cos, sa, sb):
    return x * cos + pltpu.roll(x, HD - ROPE_F, axis=1) * sa + pltpu.roll(x, ROPE_F, axis=1) * sb


def _softmax_pv(q4, k_b, v_b):
    s = lax.dot_general(q4.astype(BF16), k_b, (((1,), (1,)), ((), ())), preferred_element_type=F32)
    s = s * (HD ** -0.5)
    m = jnp.max(s, axis=-1, keepdims=True)
    p = jnp.exp(s - m)
    l = jnp.sum(p, axis=-1, keepdims=True)
    o = jnp.dot(p.astype(BF16), v_b, preferred_element_type=F32)
    return o / l


def _attn_ctx_kernel(q_ref, k_ref, v_ref, qn_ref, kn_ref, o_ref, ko_ref, vo_ref):
    k = _rms(k_ref[...], kn_ref[...])
    v = v_ref[...]
    ko_ref[...] = k
    vo_ref[...] = v
    q = q_ref[...]
    q4 = jnp.concatenate([_rms(q[:, h * HD:(h + 1) * HD], qn_ref[...]) for h in range(REP)], axis=0)
    o = _softmax_pv(q4, k.astype(BF16), v.astype(BF16))
    for h in range(REP):
        o_ref[:, h * HD:(h + 1) * HD] = o[h * SEQ:(h + 1) * SEQ].astype(BF16)


def _attn_ctx(proj, q_norm, k_norm):
    qw = REP * HD
    kv_out = jax.ShapeDtypeStruct((BATCH, SEQ, KV_DIM), F32)
    return pl.pallas_call(
        _attn_ctx_kernel,
        grid=(BATCH, H_KV),
        in_specs=[pl.BlockSpec((SEQ, qw), lambda b, g: (b, COL_Q // qw + g)),
                  pl.BlockSpec((SEQ, HD), lambda b, g: (b, COL_K // HD + g)),
                  pl.BlockSpec((SEQ, HD), lambda b, g: (b, COL_V // HD + g)),
                  pl.BlockSpec((1, HD), lambda b, g: (0, 0)),
                  pl.BlockSpec((1, HD), lambda b, g: (0, 0))],
        out_specs=[pl.BlockSpec((SEQ, qw), lambda b, g: (b, g)),
                   pl.BlockSpec((None, SEQ, HD), lambda b, g: (b, 0, g)),
                   pl.BlockSpec((None, SEQ, HD), lambda b, g: (b, 0, g))],
        out_shape=[jax.ShapeDtypeStruct((N_TOK, Q_DIM), BF16), kv_out, kv_out],
        compiler_params=_params(("arbitrary", "arbitrary")),
        name="attn_ctx",
    )(proj, proj, proj, q_norm, k_norm)


def _attn_lat_kernel(q_ref, k_ref, v_ref, ck_ref, cv_ref, cos_ref, sa_ref, sb_ref, qn_ref, kn_ref, prev_ref,
                     o_ref, kall, vall, *, tq):
    del prev_ref
    qi = pl.program_id(2)

    @pl.when(qi == 0)
    def _():
        k = _rope(_rms(k_ref[...], kn_ref[...]), cos_ref[...], sa_ref[...], sb_ref[...])
        kall[0:PAST_LEN, :] = ck_ref[...].astype(BF16)
        kall[PAST_LEN:, :] = k.astype(BF16)
        vall[0:PAST_LEN, :] = cv_ref[...].astype(BF16)
        vall[PAST_LEN:, :] = v_ref[...].astype(BF16)

    r0 = pl.multiple_of(qi * tq, tq)
    cos = cos_ref[pl.ds(r0, tq), :]
    sa = sa_ref[pl.ds(r0, tq), :]
    sb = sb_ref[pl.ds(r0, tq), :]
    q = q_ref[...]
    q4 = jnp.concatenate(
        [_rope(_rms(q[:, h * HD:(h + 1) * HD], qn_ref[...]), cos, sa, sb) for h in range(REP)], axis=0)
    o = _softmax_pv(q4, kall[...], vall[...])
    for h in range(REP):
        o_ref[:, h * HD:(h + 1) * HD] = o[h * tq:(h + 1) * tq].astype(BF16)


def _attn_lat(proj, cache_k, cache_v, cos, sa, sb, q_norm, k_norm, attn_prev):
    tq = 256
    qw = REP * HD
    nq = DEC_SEQ // tq
    row0 = N_CTX // tq
    tab = pl.BlockSpec((DEC_SEQ, HD), lambda b, g, i: (0, 0))
    return pl.pallas_call(
        functools.partial(_attn_lat_kernel, tq=tq),
        grid=(DEC_BATCH, H_KV, nq),
        in_specs=[pl.BlockSpec((tq, qw), lambda b, g, i: (row0 + b * nq + i, COL_Q // qw + g)),
                  pl.BlockSpec((DEC_SEQ, HD), lambda b, g, i: (N_CTX // DEC_SEQ + b, COL_K // HD + g)),
                  pl.BlockSpec((DEC_SEQ, HD), lambda b, g, i: (N_CTX // DEC_SEQ + b, COL_V // HD + g)),
                  pl.BlockSpec((None, PAST_LEN, HD), lambda b, g, i: (b, 0, g)),
                  pl.BlockSpec((None, PAST_LEN, HD), lambda b, g, i: (b, 0, g)),
                  tab, tab, tab,
                  pl.BlockSpec((1, HD), lambda b, g, i: (0, 0)),
                  pl.BlockSpec((1, HD), lambda b, g, i: (0, 0)),
                  pl.BlockSpec(memory_space=pl.ANY)],
        out_specs=pl.BlockSpec((tq, qw), lambda b, g, i: (row0 + b * nq + i, g)),
        out_shape=jax.ShapeDtypeStruct((N_TOK, Q_DIM), BF16),
        input_output_aliases={10: 0},
        scratch_shapes=[pltpu.VMEM((PAST_LEN + DEC_SEQ, HD), BF16),
                        pltpu.VMEM((PAST_LEN + DEC_SEQ, HD), BF16)],
        compiler_params=_params(("arbitrary", "arbitrary", "arbitrary")),
        name="attn_lat",
    )(proj, proj, proj, cache_k, cache_v, cos, sa, sb, q_norm, k_norm, attn_prev)


def _ssd_kernel(*refs, seq, has_h0, emit_state, has_prev):
    (x_ref, b_ref, c_ref, dt_ref, acs_ref, cwx_ref, cwb_ref, cwc_ref,
     cbx_ref, cbb_ref, cbc_ref, dsk_ref) = refs[:12]
    pos = 12
    h0_ref = None
    if has_h0:
        h0_ref = refs[pos]
        pos += 1
    if has_prev:
        pos += 1
    y_ref = refs[pos]
    pos += 1
    hout_ref = None
    if emit_state:
        hout_ref = refs[pos]
        pos += 1
    pad, xst, btf, cbf, csb, acc, ht = refs[pos:]
    nc = seq // CHUNK
    half = SSD_R * SSD_P // 2
    halo = 8

    ii = lax.broadcasted_iota(jnp.int32, (CHUNK, CHUNK), 0)
    jj = lax.broadcasted_iota(jnp.int32, (CHUNK, CHUNK), 1)
    lane_lo = jj < SSD_P

    def pair(a, b):
        return jnp.where(lane_lo, a, b)

    pad[0:halo, :] = jnp.zeros((halo, 4 * LANE), F32)
    pad[halo + seq:, :] = jnp.zeros((halo, 4 * LANE), F32)
    pad[halo:halo + seq, 0:2 * LANE] = x_ref[...]
    pad[halo:halo + seq, 2 * LANE:3 * LANE] = b_ref[...]
    pad[halo:halo + seq, 3 * LANE:] = c_ref[...]

    segs = [(cwx_ref, cbx_ref, 0), (cwx_ref, cbx_ref, LANE), (cwb_ref, cbb_ref, 0), (cwc_ref, cbc_ref, 0)]
    for rb in range(nc):
        vals = []
        for blk, (w_ref, bias_ref, lo) in enumerate(segs):
            a = None
            for k in range(CONV_K):
                s = halo + rb * CHUNK + k - CONV_K // 2
                term = pad[s:s + CHUNK, blk * LANE:(blk + 1) * LANE] * w_ref[k:k + 1, lo:lo + LANE]
                a = term if a is None else a + term
            a = a + bias_ref[:, lo:lo + LANE]
            vals.append(a * jax.nn.sigmoid(a))
        for p in range(2):
            v = vals[p]
            y_ref[rb * CHUNK:(rb + 1) * CHUNK, p * LANE:(p + 1) * LANE] = v * dsk_ref[:, p * LANE:(p + 1) * LANE]
            xst[rb, p, 0:CHUNK, :] = jnp.where(lane_lo, v, 0.0).astype(BF16)
            xst[rb, p, CHUNK:, :] = jnp.where(lane_lo, 0.0, v).astype(BF16)
        vb = vals[2]
        vc = vals[3].astype(BF16)
        btf[rb] = vb.T
        csb[rb] = vc
        cbf[rb] = lax.dot_general(vc, vb.astype(BF16), (((1,), (1,)), ((), ())), preferred_element_type=F32)
        acc[rb] = jnp.concatenate([acs_ref[rb]] * (CHUNK // 8), axis=0).T

    for d in range(2):
        if has_h0:
            ht[d, :, 0:half] = h0_ref[d, 0:half, :].T
            ht[d, :, half:] = h0_ref[d, half:, :].T
        else:
            ht[d] = jnp.zeros((SSD_N, 2 * half), F32)

    def chunk(c, d):
        r0 = pl.multiple_of(c * CHUNK, CHUNK)
        cbm = cbf[c]
        bt = btf[c]
        cc = csb[c]
        acol = acc[c]
        arow = acs_ref[c]
        drow = dt_ref[c]
        h_prev = ht[d]
        yoff = jnp.dot(cc, h_prev.astype(BF16), preferred_element_type=F32)
        mask = (ii >= jj) if d == 0 else (ii <= jj)
        last = CHUNK - 1 if d == 0 else 0
        ms, bws, eas, cds = [], [], [], []
        for r in range(SSD_R):
            col = d * SSD_R + r
            acb = jnp.broadcast_to(acol[:, col:col + 1], (CHUNK, CHUNK))
            ar = arow[col:col + 1, :]
            dr = drow[col:col + 1, :]
            seg = acb - ar
            dec = jnp.where(mask, jnp.exp(jnp.where(mask, seg, 0.0)), 0.0)
            ms.append((cbm * dec * dr).astype(BF16))
            eas.append(jnp.exp(acb))
            a_last = ar[:, last:last + 1]
            bws.append((bt * (dr * jnp.exp(a_last - ar))).astype(BF16))
            cds.append(jnp.exp(a_last))
        ys, sts, cdm = [], [], []
        for p in range(2):
            r_a, r_b = 2 * p, 2 * p + 1
            x2 = xst[c, p]
            yd = jnp.dot(jnp.concatenate([ms[r_a], ms[r_b]], axis=1), x2, preferred_element_type=F32)
            ys.append(yd + yoff[:, p * LANE:(p + 1) * LANE] * pair(eas[r_a], eas[r_b]))
            sts.append(jnp.dot(jnp.concatenate([bws[r_a], bws[r_b]], axis=1), x2, preferred_element_type=F32))
            cdm.append(pair(cds[r_a], cds[r_b]))
        ht[d] = h_prev * jnp.concatenate(cdm, axis=1) + jnp.concatenate(sts, axis=1)
        y_ref[pl.ds(r0, CHUNK), :] += jnp.concatenate(ys, axis=1)

    def body(t, carry):
        chunk(t, 0)
        chunk(nc - 1 - t, 1)
        return carry

    lax.fori_loop(0, nc, body, 0)

    if emit_state:
        for d in range(2):
            hout_ref[d, 0:half, :] = ht[d, :, 0:half].T
            hout_ref[d, half:, :] = ht[d, :, half:].T


def _ssd(proj, dt_t, acs_t, conv_w, conv_b, dskip, h0, y_prev, *, seq, nbatch, row_blk0, emit_state):
    gw = SSD_R * SSD_P
    nc = seq // CHUNK
    has_h0, has_prev = h0 is not None, y_prev is not None
    xb0, bb0, cb0 = COL_X // gw, COL_B // SSD_N, COL_C // SSD_N
    in_specs = [
        pl.BlockSpec((seq, gw), lambda b, g: (row_blk0 + b, xb0 + g)),
        pl.BlockSpec((seq, SSD_N), lambda b, g: (row_blk0 + b, bb0 + g)),
        pl.BlockSpec((seq, SSD_N), lambda b, g: (row_blk0 + b, cb0 + g)),
        pl.BlockSpec((nc, 8, CHUNK), lambda b, g: (row_blk0 + b, g, 0)),
        pl.BlockSpec((nc, 8, CHUNK), lambda b, g: (row_blk0 + b, g, 0)),
        pl.BlockSpec((CONV_K, gw), lambda b, g: (0, g)),
        pl.BlockSpec((CONV_K, SSD_N), lambda b, g: (0, D_INNER // SSD_N + g)),
        pl.BlockSpec((CONV_K, SSD_N), lambda b, g: (0, (D_INNER + BC_DIM) // SSD_N + g)),
        pl.BlockSpec((1, gw), lambda b, g: (0, g)),
        pl.BlockSpec((1, SSD_N), lambda b, g: (0, D_INNER // SSD_N + g)),
        pl.BlockSpec((1, SSD_N), lambda b, g: (0, (D_INNER + BC_DIM) // SSD_N + g)),
        pl.BlockSpec((1, gw), lambda b, g: (0, g)),
    ]
    args = [proj, proj, proj, dt_t, acs_t, conv_w, conv_w, conv_w, conv_b, conv_b, conv_b, dskip]
    state_spec = pl.BlockSpec((None, 2, gw, SSD_N), lambda b, g: (b, 0, g, 0))
    aliases = {}
    if has_h0:
        in_specs.append(state_spec)
        args.append(h0)
    if has_prev:
        aliases = {len(args): 0}
        in_specs.append(pl.BlockSpec(memory_space=pl.ANY))
        args.append(y_prev)
    out_specs = [pl.BlockSpec((seq, gw), lambda b, g: (row_blk0 + b, g))]
    out_shape = [jax.ShapeDtypeStruct((N_TOK, D_INNER), F32)]
    if emit_state:
        out_specs.append(state_spec)
        out_shape.append(jax.ShapeDtypeStruct((nbatch, 2, H_SSD * SSD_P, SSD_N), F32))
    return pl.pallas_call(
        functools.partial(_ssd_kernel, seq=seq, has_h0=has_h0, emit_state=emit_state, has_prev=has_prev),
        grid=(nbatch, SSD_G),
        in_specs=in_specs,
        out_specs=out_specs,
        out_shape=out_shape,
        input_output_aliases=aliases,
        scratch_shapes=[pltpu.VMEM((seq + 16, 4 * LANE), F32),
                        pltpu.VMEM((nc, 2, 2 * CHUNK, LANE), BF16),
                        pltpu.VMEM((nc, SSD_N, CHUNK), F32),
                        pltpu.VMEM((nc, CHUNK, CHUNK), F32),
                        pltpu.VMEM((nc, CHUNK, SSD_N), BF16),
                        pltpu.VMEM((nc, CHUNK, LANE), F32),
                        pltpu.VMEM((2, SSD_N, gw), F32)],
        compiler_params=_params(("arbitrary", "arbitrary")),
        name="ssd_lat" if has_h0 else "ssd_ctx",
    )(*args)


def _merge_kernel(attn_ref, y_ref, z_ref, g_ref, xp_ref, xl_ref, mod_ref, nrm_ref, wa_ref, ws_ref, wo_ref,
                  l1g_ref, l1b_ref, o_ref, *, tm):
    z = z_ref[...]
    yg = y_ref[...] * (z * jax.nn.sigmoid(z))
    yn = yg * lax.rsqrt(jnp.mean(yg * yg, axis=-1, keepdims=True) + EPS) * nrm_ref[...]
    pa = jnp.dot(attn_ref[...], wa_ref[...], preferred_element_type=F32)
    ps = jnp.dot(yn.astype(BF16), ws_ref[...], preferred_element_type=F32)
    g = g_ref[...]
    merged = jax.nn.sigmoid(g[:, :D_MODEL]) * pa + jax.nn.sigmoid(g[:, D_MODEL:]) * ps
    mix = jnp.dot(merged.astype(BF16), wo_ref[...], preferred_element_type=F32)
    x = jnp.where(pl.program_id(0) < N_CTX // tm, xp_ref[...], xl_ref[...])
    x1 = _ln(ALPHA * x + mod_ref[2:3, :] * mix)
    o_ref[...] = x1 * l1g_ref[...] + l1b_ref[...]


def _merge(attn, yssd, proj, xp, xl, mod, ssd_norm, wa, ws, wo, l1g, l1b):
    tm = 256
    const = lambda shape: pl.BlockSpec(shape, lambda i: (0, 0))
    return pl.pallas_call(
        functools.partial(_merge_kernel, tm=tm),
        grid=(N_TOK // tm,),
        in_specs=[pl.BlockSpec((tm, Q_DIM), lambda i: (i, 0)),
                  pl.BlockSpec((tm, D_INNER), lambda i: (i, 0)),
                  pl.BlockSpec((tm, D_INNER), lambda i: (i, COL_Z // D_INNER)),
                  pl.BlockSpec((tm, 2 * D_MODEL), lambda i: (i, COL_GATE // (2 * D_MODEL))),
                  pl.BlockSpec((tm, D_MODEL), lambda i: (_ctx_blk(i, tm), 0)),
                  pl.BlockSpec((tm, D_MODEL), lambda i: (_lat_blk(i, tm), 0)),
                  pl.BlockSpec((None, 6, D_MODEL), lambda i: (_mod_row(i, tm), 0, 0)),
                  const((1, D_INNER)), const((Q_DIM, D_MODEL)), const((D_INNER, D_MODEL)),
                  const((D_MODEL, D_MODEL)), const((1, D_MODEL)), const((1, D_MODEL))],
        out_specs=pl.BlockSpec((tm, D_MODEL), lambda i: (i, 0)),
        out_shape=jax.ShapeDtypeStruct((N_TOK, D_MODEL), F32),
        compiler_params=_params(("arbitrary",)),
        name="merge",
    )(attn, yssd, proj, proj, xp, xl, mod, ssd_norm, wa, ws, wo, l1g, l1b)


def _mlp_kernel(x1_ref, mod_ref, w1_ref, b1_ref, w2_ref, b2_ref, l2g_ref, l2b_ref, op_ref, ol_ref,
                h_scr, acc_scr, *, tm):
    i = pl.program_id(0)
    j = pl.program_id(1)

    @pl.when(j == 0)
    def _():
        h = _ln(x1_ref[...]) * (1.0 + mod_ref[4:5, :]) + mod_ref[3:4, :]
        h_scr[...] = h.astype(BF16)
        acc_scr[...] = jnp.zeros_like(acc_scr)

    u = jnp.dot(h_scr[...], w1_ref[...], preferred_element_type=F32) + b1_ref[...]
    u = jnp.maximum(u, 0.0)
    acc_scr[...] += jnp.dot((u * u).astype(BF16), w2_ref[...], preferred_element_type=F32)

    def result():
        ff = acc_scr[...] + b2_ref[...]
        x2 = _ln(ALPHA * x1_ref[...] + mod_ref[5:6, :] * ff)
        return x2 * l2g_ref[...] + l2b_ref[...]

    last = j == pl.num_programs(1) - 1
    is_ctx = i < N_CTX // tm

    @pl.when(last & is_ctx)
    def _():
        op_ref[...] = result()

    @pl.when(last & jnp.logical_not(is_ctx))
    def _():
        ol_ref[...] = result()


def _mlp(x1, mod, w1, b1, w2, b2, l2g, l2b):
    tm, tf = 512, 1024
    return pl.pallas_call(
        functools.partial(_mlp_kernel, tm=tm),
        grid=(N_TOK // tm, D_FF // tf),
        in_specs=[pl.BlockSpec((tm, D_MODEL), lambda i, j: (i, 0)),
                  pl.BlockSpec((None, 6, D_MODEL), lambda i, j: (_mod_row(i, tm), 0, 0)),
                  pl.BlockSpec((D_MODEL, tf), lambda i, j: (0, j)),
                  pl.BlockSpec((1, tf), lambda i, j: (0, j)),
                  pl.BlockSpec((tf, D_MODEL), lambda i, j: (j, 0)),
                  pl.BlockSpec((1, D_MODEL), lambda i, j: (0, 0)),
                  pl.BlockSpec((1, D_MODEL), lambda i, j: (0, 0)),
                  pl.BlockSpec((1, D_MODEL), lambda i, j: (0, 0))],
        out_specs=[pl.BlockSpec((tm, D_MODEL), lambda i, j: (_ctx_blk(i, tm), 0)),
                   pl.BlockSpec((tm, D_MODEL), lambda i, j: (_lat_blk(i, tm), 0))],
        out_shape=[jax.ShapeDtypeStruct((N_CTX, D_MODEL), F32), jax.ShapeDtypeStruct((N_LAT, D_MODEL), F32)],
        scratch_shapes=[pltpu.VMEM((tm, D_MODEL), BF16), pltpu.VMEM((tm, D_MODEL), F32)],
        compiler_params=_params(("arbitrary", "arbitrary")),
        name="mlp",
    )(x1, mod, w1, b1, w2, b2, l2g, l2b)


def _dt_perm():
    idx = []
    for n in range(DT_DIM):
        g, d, r = n // 8, (n % 8) // SSD_R, n % SSD_R
        idx.append(d * H_SSD + g * SSD_R + r)
    return jnp.asarray(idx, dtype=jnp.int32)


def _rope_tables():
    t = jnp.arange(DEC_SEQ)
    posn = jnp.stack([t // GRID_W, t % GRID_W], axis=-1).astype(F32)
    inv = ROPE_THETA ** (-jnp.arange(ROPE_F, dtype=F32) / ROPE_F)
    ang = posn[:, :, None] * inv
    cos, sin = jnp.cos(ang), jnp.sin(ang)
    zero = jnp.zeros_like(sin[:, 0])
    cos_t = jnp.concatenate([cos[:, 0], cos[:, 0], cos[:, 1], cos[:, 1]], axis=-1)
    sa_t = jnp.concatenate([-sin[:, 0], zero, -sin[:, 1], zero], axis=-1)
    sb_t = jnp.concatenate([zero, sin[:, 0], zero, sin[:, 1]], axis=-1)
    return cos_t, sa_t, sb_t


def kernel(x_prompt, x_sample, cache_k, cache_v, state_ssd, c, c_ctx, w_mod, b_mod, w_in, q_norm, k_norm, conv_w, conv_b, a_log, dt_bias, d_skip, ssd_norm, w_attn_o, w_ssd_o, w_out, ln1_g, ln1_b, w_mlp1, b_mlp1, w_mlp2, b_mlp2, ln2_g, ln2_b):
    l = 0
    perm = _dt_perm()
    w = w_in[l]
    o_q, o_k, o_v, o_z = 0, Q_DIM, Q_DIM + KV_DIM, Q_DIM + 2 * KV_DIM
    o_xbc = o_z + D_INNER
    o_dt = o_xbc + XBC_DIM
    o_gate = o_dt + DT_DIM
    w_dt = jnp.take(w[:, o_dt:o_dt + DT_DIM], perm, axis=1)
    w_in_b = jnp.concatenate(
        [w[:, o_z:o_z + D_INNER], w[:, o_gate:o_gate + 2 * D_MODEL], w[:, o_xbc:o_xbc + XBC_DIM],
         w[:, o_q:o_q + Q_DIM], w[:, o_k:o_k + KV_DIM], w[:, o_v:o_v + KV_DIM], w_dt,
         jnp.zeros((D_MODEL, LANE - DT_DIM), F32)], axis=1).astype(BF16)
    pad = jnp.zeros((LANE - DT_DIM,), F32)
    bias_col = jnp.concatenate([jnp.take(dt_bias[l].reshape(-1), perm), pad]).reshape(LANE, 1)
    alog_col = jnp.concatenate([jnp.take(a_log[l].reshape(-1), perm), pad]).reshape(LANE, 1)
    dskip_row = jnp.repeat(d_skip[l], SSD_P).reshape(1, D_INNER)
    row = lambda v: v.reshape(1, -1)

    c_all = jnp.concatenate([c_ctx[None, :], c, jnp.zeros((8 - 1 - DEC_BATCH, D_MODEL), F32)], axis=0)
    mod = _modulation(c_all, w_mod[l], row(b_mod[l])).reshape(8, 6, D_MODEL)

    xp = x_prompt.reshape(N_CTX, D_MODEL)
    xl = x_sample.reshape(N_LAT, D_MODEL)
    proj = _inproj(xp, xl, mod, w_in_b)
    dt_t, acs_t = _dtprep(proj, bias_col, alog_col)

    qn, kn = row(q_norm[l]), row(k_norm[l])
    attn, k_new, v_new = _attn_ctx(proj, qn, kn)
    cos_t, sa_t, sb_t = _rope_tables()
    attn = _attn_lat(proj, cache_k[:, l].reshape(DEC_BATCH, PAST_LEN, KV_DIM),
                     cache_v[:, l].reshape(DEC_BATCH, PAST_LEN, KV_DIM), cos_t, sa_t, sb_t, qn, kn, attn)

    cw, cb = conv_w[l], row(conv_b[l])
    yssd, h_new = _ssd(proj, dt_t, acs_t, cw, cb, dskip_row, None, None, seq=SEQ, nbatch=BATCH,
                       row_blk0=0, emit_state=True)
    h0 = state_ssd[:, l].reshape(DEC_BATCH, 2, H_SSD * SSD_P, SSD_N)
    (yssd,) = _ssd(proj, dt_t, acs_t, cw, cb, dskip_row, h0, yssd, seq=DEC_SEQ, nbatch=DEC_BATCH,
                   row_blk0=N_CTX // DEC_SEQ, emit_state=False)

    x1 = _merge(attn, yssd, proj, xp, xl, mod, row(ssd_norm[l]), w_attn_o[l].astype(BF16),
                w_ssd_o[l].astype(BF16), w_out[l].astype(BF16), row(ln1_g[l]), row(ln1_b[l]))
    out_p, out_l = _mlp(x1, mod, w_mlp1[l].astype(BF16), row(b_mlp1[l]), w_mlp2[l].astype(BF16),
                        row(b_mlp2[l]), row(ln2_g[l]), row(ln2_b[l]))

    y_p = out_p.reshape(BATCH, SEQ, D_MODEL)
    y_s = out_l.reshape(DEC_BATCH, DEC_SEQ, D_MODEL)
    new_cache_k = k_new.reshape(BATCH, DEPTH, SEQ, H_KV, HD)
    new_cache_v = v_new.reshape(BATCH, DEPTH, SEQ, H_KV, HD)
    new_state = h_new.reshape(BATCH, DEPTH, 2, H_SSD, SSD_P, SSD_N)
    return (y_p, y_s, new_cache_k, new_cache_v, new_state)
```

```python
import functools

import jax
import jax.numpy as jnp
from jax import lax
from jax.experimental import pallas as pl
from jax.experimental.pallas import tpu as pltpu

D_MODEL = 1024
BATCH = 16
SEQ = 256
DEC_BATCH = 4
DEC_SEQ = 1024
PAST_LEN = 512
GRID_W = 64
H_Q = 8
H_KV = 2
HD = 128
ROPE_F = HD // 4
ROPE_THETA = 10000.0
D_INNER = 2 * D_MODEL
SSD_P = 64
H_SSD = D_INNER // SSD_P
SSD_G = 8
SSD_R = H_SSD // SSD_G
SSD_N = 128
CONV_K = 5
CHUNK = 128
D_FF = 4 * D_MODEL
DEPTH = 1
ALPHA = (2.0 * DEPTH) ** 0.25
EPS = 1e-6
Q_DIM = H_Q * HD
KV_DIM = H_KV * HD
BC_DIM = SSD_G * SSD_N
XBC_DIM = D_INNER + 2 * BC_DIM
DT_DIM = 2 * H_SSD
REP = H_Q // H_KV

N_CTX = BATCH * SEQ
N_LAT = DEC_BATCH * DEC_SEQ
N_TOK = N_CTX + N_LAT

LANE = 128

COL_Z = 0
COL_GATE = COL_Z + D_INNER
COL_X = COL_GATE + 2 * D_MODEL
COL_B = COL_X + D_INNER
COL_C = COL_B + BC_DIM
COL_Q = COL_C + BC_DIM
COL_K = COL_Q + Q_DIM
COL_V = COL_K + KV_DIM
COL_DT = COL_V + KV_DIM
PROJ_W = COL_DT + LANE

VMEM_LIMIT = 52 * 1024 * 1024

F32 = jnp.float32
BF16 = jnp.bfloat16


def _ln(x):
    mu = jnp.mean(x, axis=-1, keepdims=True)
    xc = x - mu
    var = jnp.mean(xc * xc, axis=-1, keepdims=True)
    return xc * lax.rsqrt(var + EPS)


def _rms(x, g):
    return x * lax.rsqrt(jnp.mean(x * x, axis=-1, keepdims=True) + EPS) * g


def _mod_row(i, tm):
    nct = N_CTX // tm
    tpl = DEC_SEQ // tm
    return jnp.where(i < nct, 0, 1 + jnp.maximum(i - nct, 0) // tpl)


def _ctx_blk(i, tm):
    return jnp.minimum(i, N_CTX // tm - 1)


def _lat_blk(i, tm):
    return jnp.maximum(i - N_CTX // tm, 0)


def _params(sem):
    return pltpu.CompilerParams(dimension_semantics=sem, vmem_limit_bytes=VMEM_LIMIT)


def _mod_kernel(c_ref, w_ref, b_ref, o_ref):
    c = c_ref[...]
    s = c * jax.nn.sigmoid(c)
    o_ref[...] = jnp.dot(s.astype(BF16), w_ref[...].astype(BF16), preferred_element_type=F32) + b_ref[...]


def _modulation(c_all, w_mod, b_mod):
    tn = 1536
    return pl.pallas_call(
        _mod_kernel,
        grid=(6 * D_MODEL // tn,),
        in_specs=[pl.BlockSpec((8, D_MODEL), lambda j: (0, 0)),
                  pl.BlockSpec((D_MODEL, tn), lambda j: (0, j)),
                  pl.BlockSpec((1, tn), lambda j: (0, j))],
        out_specs=pl.BlockSpec((8, tn), lambda j: (0, j)),
        out_shape=jax.ShapeDtypeStruct((8, 6 * D_MODEL), F32),
        compiler_params=_params(("arbitrary",)),
        name="modulation",
    )(c_all, w_mod, b_mod)


def _inproj_kernel(xp_ref, xl_ref, mod_ref, w_ref, wdt_ref, o_ref, dt_ref, h_scr, *, tm):
    @pl.when(pl.program_id(1) == 0)
    def _():
        x = jnp.where(pl.program_id(0) < N_CTX // tm, xp_ref[...], xl_ref[...])
        h = _ln(x) * (1.0 + mod_ref[1:2, :]) + mod_ref[0:1, :]
        h_scr[...] = h.astype(BF16)
        dt_ref[...] = jnp.dot(h_scr[...], wdt_ref[...], preferred_element_type=F32)

    o_ref[...] = jnp.dot(h_scr[...], w_ref[...], preferred_element_type=F32).astype(BF16)


def _inproj(xp, xl, mod, w_main, w_dt):
    tm, tn = 1024, 1408
    return pl.pallas_call(
        functools.partial(_inproj_kernel, tm=tm),
        grid=(N_TOK // tm, PROJ_W // tn),
        in_specs=[pl.BlockSpec((tm, D_MODEL), lambda i, j: (_ctx_blk(i, tm), 0)),
                  pl.BlockSpec((tm, D_MODEL), lambda i, j: (_lat_blk(i, tm), 0)),
                  pl.BlockSpec((None, 6, D_MODEL), lambda i, j: (_mod_row(i, tm), 0, 0)),
                  pl.BlockSpec((D_MODEL, tn), lambda i, j: (0, j)),
                  pl.BlockSpec((D_MODEL, LANE), lambda i, j: (0, 0))],
        out_specs=[pl.BlockSpec((tm, tn), lambda i, j: (i, j)),
                   pl.BlockSpec((tm, LANE), lambda i, j: (i, 0))],
        out_shape=[jax.ShapeDtypeStruct((N_TOK, PROJ_W), BF16),
                   jax.ShapeDtypeStruct((N_TOK, LANE), F32)],
        scratch_shapes=[pltpu.VMEM((tm, D_MODEL), BF16)],
        compiler_params=_params(("arbitrary", "arbitrary")),
        name="inproj",
    )(xp, xl, mod, w_main, w_dt)


def _dtprep_kernel(p_ref, bias_ref, alog_ref, dt_ref, acs_ref, *, nchunk):
    a_head = -jnp.exp(alog_ref[...])
    ii = lax.broadcasted_iota(jnp.int32, (CHUNK, CHUNK), 0)
    jj = lax.broadcasted_iota(jnp.int32, (CHUNK, CHUNK), 1)
    upper = (ii <= jj).astype(F32)
    lower = (ii >= jj).astype(F32)
    is_fwd = (ii & 7) < SSD_R
    for c in range(nchunk):
        t = p_ref[c * CHUNK:(c + 1) * CHUNK, :]
        tt = t.T + bias_ref[...]
        dtv = jnp.maximum(tt, 0.0) + jnp.log1p(jnp.exp(-jnp.abs(tt)))
        a = dtv * a_head
        pre = jnp.dot(a, upper, precision=lax.Precision.HIGHEST, preferred_element_type=F32)
        suf = jnp.dot(a, lower, precision=lax.Precision.HIGHEST, preferred_element_type=F32)
        dt_ref[c] = dtv
        acs_ref[c] = jnp.where(is_fwd, pre, suf)


def _dtprep(proj, bias_col, alog_col):
    nchunk = 4
    tm = nchunk * CHUNK
    nct = N_TOK // CHUNK
    out = jax.ShapeDtypeStruct((nct, LANE, CHUNK), F32)
    return pl.pallas_call(
        functools.partial(_dtprep_kernel, nchunk=nchunk),
        grid=(N_TOK // tm,),
        in_specs=[pl.BlockSpec((tm, LANE), lambda i: (i, 0)),
                  pl.BlockSpec((LANE, 1), lambda i: (0, 0)),
                  pl.BlockSpec((LANE, 1), lambda i: (0, 0))],
        out_specs=[pl.BlockSpec((nchunk, LANE, CHUNK), lambda i: (i, 0, 0)),
                   pl.BlockSpec((nchunk, LANE, CHUNK), lambda i: (i, 0, 0))],
        out_shape=[out, out],
        compiler_params=_params(("arbitrary",)),
        name="dtprep",
    )(proj, bias_col, alog_col)


def _rope(x, cos, sa, sb):
    return x * cos + pltpu.roll(x, HD - ROPE_F, axis=1) * sa + pltpu.roll(x, ROPE_F, axis=1) * sb


def _softmax_pv(q4, k_b, v_b):
    s = lax.dot_general(q4.astype(BF16), k_b, (((1,), (1,)), ((), ())), preferred_element_type=F32)
    m = jnp.max(s, axis=-1, keepdims=True)
    p = jnp.exp(s - m)
    l = jnp.sum(p, axis=-1, keepdims=True)
    o = jnp.dot(p.astype(BF16), v_b, preferred_element_type=F32)
    return o / l


def _attn_ctx_kernel(q_ref, k_ref, v_ref, qn_ref, kn_ref, o_ref, ko_ref, vo_ref):
    k = _rms(k_ref[...].astype(F32), kn_ref[...])
    v = v_ref[...]
    ko_ref[...] = k
    vo_ref[...] = v.astype(F32)
    q = q_ref[...].astype(F32)
    qg = qn_ref[...] * (HD ** -0.5)
    q4 = jnp.concatenate([_rms(q[:, h * HD:(h + 1) * HD], qg) for h in range(REP)], axis=0)
    o = _softmax_pv(q4, k.astype(BF16), v)
    for h in range(REP):
        o_ref[:, h * HD:(h + 1) * HD] = o[h * SEQ:(h + 1) * SEQ].astype(BF16)


def _attn_ctx(proj, q_norm, k_norm):
    qw = REP * HD
    kv_out = jax.ShapeDtypeStruct((BATCH, SEQ, KV_DIM), F32)
    return pl.pallas_call(
        _attn_ctx_kernel,
        grid=(BATCH, H_KV),
        in_specs=[pl.BlockSpec((SEQ, qw), lambda b, g: (b, COL_Q // qw + g)),
                  pl.BlockSpec((SEQ, HD), lambda b, g: (b, COL_K // HD + g)),
                  pl.BlockSpec((SEQ, HD), lambda b, g: (b, COL_V // HD + g)),
                  pl.BlockSpec((1, HD), lambda b, g: (0, 0)),
                  pl.BlockSpec((1, HD), lambda b, g: (0, 0))],
        out_specs=[pl.BlockSpec((SEQ, qw), lambda b, g: (b, g)),
                   pl.BlockSpec((None, SEQ, HD), lambda b, g: (b, 0, g)),
                   pl.BlockSpec((None, SEQ, HD), lambda b, g: (b, 0, g))],
        out_shape=[jax.ShapeDtypeStruct((N_TOK, Q_DIM), BF16), kv_out, kv_out],
        compiler_params=_params(("arbitrary", "arbitrary")),
        name="attn_ctx",
    )(proj, proj, proj, q_norm, k_norm)


def _attn_lat_kernel(q_ref, k_ref, v_ref, ck_ref, cv_ref, cos_ref, sa_ref, sb_ref, qn_ref, kn_ref, prev_ref,
                     o_ref, kall, vall, *, tq):
    del prev_ref
    qi = pl.program_id(2)

    @pl.when(qi == 0)
    def _():
        k = _rope(_rms(k_ref[...].astype(F32), kn_ref[...]), cos_ref[...], sa_ref[...], sb_ref[...])
        kall[0:PAST_LEN, :] = ck_ref[...].astype(BF16)
        kall[PAST_LEN:, :] = k.astype(BF16)
        vall[0:PAST_LEN, :] = cv_ref[...].astype(BF16)
        vall[PAST_LEN:, :] = v_ref[...]

    r0 = pl.multiple_of(qi * tq, tq)
    cos = cos_ref[pl.ds(r0, tq), :]
    sa = sa_ref[pl.ds(r0, tq), :]
    sb = sb_ref[pl.ds(r0, tq), :]
    q = q_ref[...].astype(F32)
    qg = qn_ref[...] * (HD ** -0.5)
    q4 = jnp.concatenate(
        [_rope(_rms(q[:, h * HD:(h + 1) * HD], qg), cos, sa, sb) for h in range(REP)], axis=0)
    o = _softmax_pv(q4, kall[...], vall[...])
    for h in range(REP):
        o_ref[:, h * HD:(h + 1) * HD] = o[h * tq:(h + 1) * tq].astype(BF16)


def _attn_lat(proj, cache_k, cache_v, cos, sa, sb, q_norm, k_norm, attn_prev):
    tq = 256
    qw = REP * HD
    nq = DEC_SEQ // tq
    row0 = N_CTX // tq
    tab = pl.BlockSpec((DEC_SEQ, HD), lambda b, g, i: (0, 0))
    return pl.pallas_call(
        functools.partial(_attn_lat_kernel, tq=tq),
        grid=(DEC_BATCH, H_KV, nq),
        in_specs=[pl.BlockSpec((tq, qw), lambda b, g, i: (row0 + b * nq + i, COL_Q // qw + g)),
                  pl.BlockSpec((DEC_SEQ, HD), lambda b, g, i: (N_CTX // DEC_SEQ + b, COL_K // HD + g)),
                  pl.BlockSpec((DEC_SEQ, HD), lambda b, g, i: (N_CTX // DEC_SEQ + b, COL_V // HD + g)),
                  pl.BlockSpec((None, PAST_LEN, HD), lambda b, g, i: (b, 0, g)),
                  pl.BlockSpec((None, PAST_LEN, HD), lambda b, g, i: (b, 0, g)),
                  tab, tab, tab,
                  pl.BlockSpec((1, HD), lambda b, g, i: (0, 0)),
                  pl.BlockSpec((1, HD), lambda b, g, i: (0, 0)),
                  pl.BlockSpec(memory_space=pl.ANY)],
        out_specs=pl.BlockSpec((tq, qw), lambda b, g, i: (row0 + b * nq + i, g)),
        out_shape=jax.ShapeDtypeStruct((N_TOK, Q_DIM), BF16),
        input_output_aliases={10: 0},
        scratch_shapes=[pltpu.VMEM((PAST_LEN + DEC_SEQ, HD), BF16),
                        pltpu.VMEM((PAST_LEN + DEC_SEQ, HD), BF16)],
        compiler_params=_params(("arbitrary", "arbitrary", "arbitrary")),
        name="attn_lat",
    )(proj, proj, proj, cache_k, cache_v, cos, sa, sb, q_norm, k_norm, attn_prev)


def _ssd_kernel(*refs, seq, has_h0, emit_state, has_prev):
    (x_ref, b_ref, c_ref, dt_ref, acs_ref, cwx_ref, cwb_ref, cwc_ref,
     cbx_ref, cbb_ref, cbc_ref, dsk_ref) = refs[:12]
    pos = 12
    h0_ref = None
    if has_h0:
        h0_ref = refs[pos]
        pos += 1
    if has_prev:
        pos += 1
    y_ref = refs[pos]
    pos += 1
    hout_ref = None
    if emit_state:
        hout_ref = refs[pos]
        pos += 1
    pad, xst, btf, cbf, csb, acc, ht, yacc = refs[pos:]
    nc = seq // CHUNK
    half = SSD_R * SSD_P // 2
    halo = 8

    ii = lax.broadcasted_iota(jnp.int32, (CHUNK, CHUNK), 0)
    jj = lax.broadcasted_iota(jnp.int32, (CHUNK, CHUNK), 1)
    lane_lo = jj < SSD_P

    def pair(a, b):
        return jnp.where(lane_lo, a, b)

    pad[0:halo, :] = jnp.zeros((halo, 4 * LANE), F32)
    pad[halo + seq:, :] = jnp.zeros((halo, 4 * LANE), F32)
    pad[halo:halo + seq, 0:2 * LANE] = x_ref[...].astype(F32)
    pad[halo:halo + seq, 2 * LANE:3 * LANE] = b_ref[...].astype(F32)
    pad[halo:halo + seq, 3 * LANE:] = c_ref[...].astype(F32)

    segs = [(cwx_ref, cbx_ref, 0), (cwx_ref, cbx_ref, LANE), (cwb_ref, cbb_ref, 0), (cwc_ref, cbc_ref, 0)]
    for rb in range(nc):
        vals = []
        for blk, (w_ref, bias_ref, lo) in enumerate(segs):
            a = None
            for k in range(CONV_K):
                s = halo + rb * CHUNK + k - CONV_K // 2
                term = pad[s:s + CHUNK, blk * LANE:(blk + 1) * LANE] * w_ref[k:k + 1, lo:lo + LANE]
                a = term if a is None else a + term
            a = a + bias_ref[:, lo:lo + LANE]
            vals.append(a * jax.nn.sigmoid(a))
        for p in range(2):
            v = vals[p]
            yacc[rb * CHUNK:(rb + 1) * CHUNK, p * LANE:(p + 1) * LANE] = v * dsk_ref[:, p * LANE:(p + 1) * LANE]
            xst[rb, p, 0:CHUNK, :] = jnp.where(lane_lo, v, 0.0).astype(BF16)
            xst[rb, p, CHUNK:, :] = jnp.where(lane_lo, 0.0, v).astype(BF16)
        vb = vals[2]
        vc = vals[3].astype(BF16)
        btf[rb] = vb.T
        csb[rb] = vc
        cbf[rb] = lax.dot_general(vc, vb.astype(BF16), (((1,), (1,)), ((), ())), preferred_element_type=F32)
        acc[rb] = jnp.concatenate([acs_ref[rb]] * (CHUNK // 8), axis=0).T

    for d in range(2):
        if has_h0:
            ht[d, :, 0:half] = h0_ref[d, 0:half, :].T
            ht[d, :, half:] = h0_ref[d, half:, :].T
        else:
            ht[d] = jnp.zeros((SSD_N, 2 * half), F32)

    def chunk(c, d):
        r0 = pl.multiple_of(c * CHUNK, CHUNK)
        cbm = cbf[c]
        bt = btf[c]
        cc = csb[c]
        acol = acc[c]
        arow = acs_ref[c]
        drow = dt_ref[c]
        h_prev = ht[d]
        yoff = jnp.dot(cc, h_prev.astype(BF16), preferred_element_type=F32)
        mask = (ii >= jj) if d == 0 else (ii <= jj)
        cbm = jnp.where(mask, cbm, 0.0)
        last = CHUNK - 1 if d == 0 else 0
        ms, bws, eas, cds = [], [], [], []
        for r in range(SSD_R):
            col = d * SSD_R + r
            acb = jnp.broadcast_to(acol[:, col:col + 1], (CHUNK, CHUNK))
            ar = arow[col:col + 1, :]
            dr = drow[col:col + 1, :]
            dec = jnp.exp(jnp.minimum(acb - ar, 0.0))
            ms.append((cbm * dec * dr).astype(BF16))
            eas.append(jnp.exp(acb))
            a_last = ar[:, last:last + 1]
            bws.append((bt * (dr * jnp.exp(a_last - ar))).astype(BF16))
            cds.append(jnp.exp(a_last))
        ys, sts, cdm = [], [], []
        for p in range(2):
            r_a, r_b = 2 * p, 2 * p + 1
            x2 = xst[c, p]
            yd = jnp.dot(jnp.concatenate([ms[r_a], ms[r_b]], axis=1), x2, preferred_element_type=F32)
            ys.append(yd + yoff[:, p * LANE:(p + 1) * LANE] * pair(eas[r_a], eas[r_b]))
            sts.append(jnp.dot(jnp.concatenate([bws[r_a], bws[r_b]], axis=1), x2, preferred_element_type=F32))
            cdm.append(pair(cds[r_a], cds[r_b]))
        ht[d] = h_prev * jnp.concatenate(cdm, axis=1) + jnp.concatenate(sts, axis=1)
        yacc[pl.ds(r0, CHUNK), :] += jnp.concatenate(ys, axis=1)

    def body(t, carry):
        chunk(t, 0)
        chunk(nc - 1 - t, 1)
        return carry

    lax.fori_loop(0, nc, body, 0)
    y_ref[...] = yacc[...].astype(BF16)

    if emit_state:
        for d in range(2):
            hout_ref[d, 0:half, :] = ht[d, :, 0:half].T
            hout_ref[d, half:, :] = ht[d, :, half:].T


def _ssd(proj, dt_t, acs_t, conv_w, conv_b, dskip, h0, y_prev, *, seq, nbatch, row_blk0, emit_state):
    gw = SSD_R * SSD_P
    nc = seq // CHUNK
    has_h0, has_prev = h0 is not None, y_prev is not None
    xb0, bb0, cb0 = COL_X // gw, COL_B // SSD_N, COL_C // SSD_N
    in_specs = [
        pl.BlockSpec((seq, gw), lambda b, g: (row_blk0 + b, xb0 + g)),
        pl.BlockSpec((seq, SSD_N), lambda b, g: (row_blk0 + b, bb0 + g)),
        pl.BlockSpec((seq, SSD_N), lambda b, g: (row_blk0 + b, cb0 + g)),
        pl.BlockSpec((nc, 8, CHUNK), lambda b, g: (row_blk0 + b, g, 0)),
        pl.BlockSpec((nc, 8, CHUNK), lambda b, g: (row_blk0 + b, g, 0)),
        pl.BlockSpec((CONV_K, gw), lambda b, g: (0, g)),
        pl.BlockSpec((CONV_K, SSD_N), lambda b, g: (0, D_INNER // SSD_N + g)),
        pl.BlockSpec((CONV_K, SSD_N), lambda b, g: (0, (D_INNER + BC_DIM) // SSD_N + g)),
        pl.BlockSpec((1, gw), lambda b, g: (0, g)),
        pl.BlockSpec((1, SSD_N), lambda b, g: (0, D_INNER // SSD_N + g)),
        pl.BlockSpec((1, SSD_N), lambda b, g: (0, (D_INNER + BC_DIM) // SSD_N + g)),
        pl.BlockSpec((1, gw), lambda b, g: (0, g)),
    ]
    args = [proj, proj, proj, dt_t, acs_t, conv_w, conv_w, conv_w, conv_b, conv_b, conv_b, dskip]
    state_spec = pl.BlockSpec((None, 2, gw, SSD_N), lambda b, g: (b, 0, g, 0))
    aliases = {}
    if has_h0:
        in_specs.append(state_spec)
        args.append(h0)
    if has_prev:
        aliases = {len(args): 0}
        in_specs.append(pl.BlockSpec(memory_space=pl.ANY))
        args.append(y_prev)
    out_specs = [pl.BlockSpec((seq, gw), lambda b, g: (row_blk0 + b, g))]
    out_shape = [jax.ShapeDtypeStruct((N_TOK, D_INNER), BF16)]
    if emit_state:
        out_specs.append(state_spec)
        out_shape.append(jax.ShapeDtypeStruct((nbatch, 2, H_SSD * SSD_P, SSD_N), F32))
    return pl.pallas_call(
        functools.partial(_ssd_kernel, seq=seq, has_h0=has_h0, emit_state=emit_state, has_prev=has_prev),
        grid=(nbatch, SSD_G),
        in_specs=in_specs,
        out_specs=out_specs,
        out_shape=out_shape,
        input_output_aliases=aliases,
        scratch_shapes=[pltpu.VMEM((seq + 16, 4 * LANE), F32),
                        pltpu.VMEM((nc, 2, 2 * CHUNK, LANE), BF16),
                        pltpu.VMEM((nc, SSD_N, CHUNK), F32),
                        pltpu.VMEM((nc, CHUNK, CHUNK), F32),
                        pltpu.VMEM((nc, CHUNK, SSD_N), BF16),
                        pltpu.VMEM((nc, CHUNK, LANE), F32),
                        pltpu.VMEM((2, SSD_N, gw), F32),
                        pltpu.VMEM((seq, gw), F32)],
        compiler_params=_params(("arbitrary", "arbitrary")),
        name="ssd_lat" if has_h0 else "ssd_ctx",
    )(*args)


def _merge_kernel(attn_ref, y_ref, z_ref, g_ref, xp_ref, xl_ref, mod_ref, nrm_ref, wa_ref, ws_ref, wo_ref,
                  l1g_ref, l1b_ref, o_ref, *, tm):
    z = z_ref[...].astype(F32)
    yg = y_ref[...].astype(F32) * (z * jax.nn.sigmoid(z))
    yn = yg * lax.rsqrt(jnp.mean(yg * yg, axis=-1, keepdims=True) + EPS) * nrm_ref[...]
    pa = jnp.dot(attn_ref[...], wa_ref[...], preferred_element_type=F32)
    ps = jnp.dot(yn.astype(BF16), ws_ref[...], preferred_element_type=F32)
    g = g_ref[...].astype(F32)
    merged = jax.nn.sigmoid(g[:, :D_MODEL]) * pa + jax.nn.sigmoid(g[:, D_MODEL:]) * ps
    mix = jnp.dot(merged.astype(BF16), wo_ref[...], preferred_element_type=F32)
    x = jnp.where(pl.program_id(0) < N_CTX // tm, xp_ref[...], xl_ref[...])
    x1 = _ln(ALPHA * x + mod_ref[2:3, :] * mix)
    o_ref[...] = x1 * l1g_ref[...] + l1b_ref[...]


def _merge(attn, yssd, proj, xp, xl, mod, ssd_norm, wa, ws, wo, l1g, l1b):
    tm = 256
    const = lambda shape: pl.BlockSpec(shape, lambda i: (0, 0))
    return pl.pallas_call(
        functools.partial(_merge_kernel, tm=tm),
        grid=(N_TOK // tm,),
        in_specs=[pl.BlockSpec((tm, Q_DIM), lambda i: (i, 0)),
                  pl.BlockSpec((tm, D_INNER), lambda i: (i, 0)),
                  pl.BlockSpec((tm, D_INNER), lambda i: (i, COL_Z // D_INNER)),
                  pl.BlockSpec((tm, 2 * D_MODEL), lambda i: (i, COL_GATE // (2 * D_MODEL))),
                  pl.BlockSpec((tm, D_MODEL), lambda i: (_ctx_blk(i, tm), 0)),
                  pl.BlockSpec((tm, D_MODEL), lambda i: (_lat_blk(i, tm), 0)),
                  pl.BlockSpec((None, 6, D_MODEL), lambda i: (_mod_row(i, tm), 0, 0)),
                  const((1, D_INNER)), const((Q_DIM, D_MODEL)), const((D_INNER, D_MODEL)),
                  const((D_MODEL, D_MODEL)), const((1, D_MODEL)), const((1, D_MODEL))],
        out_specs=pl.BlockSpec((tm, D_MODEL), lambda i: (i, 0)),
        out_shape=jax.ShapeDtypeStruct((N_TOK, D_MODEL), F32),
        compiler_params=_params(("arbitrary",)),
        name="merge",
    )(attn, yssd, proj, proj, xp, xl, mod, ssd_norm, wa, ws, wo, l1g, l1b)


def _mlp_kernel(x1_ref, mod_ref, w1_ref, b1_ref, w2_ref, b2_ref, l2g_ref, l2b_ref, op_ref, ol_ref,
                h_scr, acc_scr, *, tm):
    i = pl.program_id(0)
    j = pl.program_id(1)

    @pl.when(j == 0)
    def _():
        h = _ln(x1_ref[...]) * (1.0 + mod_ref[4:5, :]) + mod_ref[3:4, :]
        h_scr[...] = h.astype(BF16)
        acc_scr[...] = jnp.zeros_like(acc_scr)

    u = jnp.dot(h_scr[...], w1_ref[...], preferred_element_type=F32) + b1_ref[...]
    u = jnp.maximum(u, 0.0)
    acc_scr[...] += jnp.dot((u * u).astype(BF16), w2_ref[...], preferred_element_type=F32)

    def result():
        ff = acc_scr[...] + b2_ref[...]
        x2 = _ln(ALPHA * x1_ref[...] + mod_ref[5:6, :] * ff)
        return x2 * l2g_ref[...] + l2b_ref[...]

    last = j == pl.num_programs(1) - 1
    is_ctx = i < N_CTX // tm

    @pl.when(last & is_ctx)
    def _():
        op_ref[...] = result()

    @pl.when(last & jnp.logical_not(is_ctx))
    def _():
        ol_ref[...] = result()


def _mlp(x1, mod, w1, b1, w2, b2, l2g, l2b):
    tm, tf = 1024, 512
    return pl.pallas_call(
        functools.partial(_mlp_kernel, tm=tm),
        grid=(N_TOK // tm, D_FF // tf),
        in_specs=[pl.BlockSpec((tm, D_MODEL), lambda i, j: (i, 0)),
                  pl.BlockSpec((None, 6, D_MODEL), lambda i, j: (_mod_row(i, tm), 0, 0)),
                  pl.BlockSpec((D_MODEL, tf), lambda i, j: (0, j)),
                  pl.BlockSpec((1, tf), lambda i, j: (0, j)),
                  pl.BlockSpec((tf, D_MODEL), lambda i, j: (j, 0)),
                  pl.BlockSpec((1, D_MODEL), lambda i, j: (0, 0)),
                  pl.BlockSpec((1, D_MODEL), lambda i, j: (0, 0)),
                  pl.BlockSpec((1, D_MODEL), lambda i, j: (0, 0))],
        out_specs=[pl.BlockSpec((tm, D_MODEL), lambda i, j: (_ctx_blk(i, tm), 0)),
                   pl.BlockSpec((tm, D_MODEL), lambda i, j: (_lat_blk(i, tm), 0))],
        out_shape=[jax.ShapeDtypeStruct((N_CTX, D_MODEL), F32), jax.ShapeDtypeStruct((N_LAT, D_MODEL), F32)],
        scratch_shapes=[pltpu.VMEM((tm, D_MODEL), BF16), pltpu.VMEM((tm, D_MODEL), F32)],
        compiler_params=_params(("arbitrary", "arbitrary")),
        name="mlp",
    )(x1, mod, w1, b1, w2, b2, l2g, l2b)


def _dt_perm():
    idx = []
    for n in range(DT_DIM):
        g, d, r = n // 8, (n % 8) // SSD_R, n % SSD_R
        idx.append(d * H_SSD + g * SSD_R + r)
    return jnp.asarray(idx, dtype=jnp.int32)


def _rope_tables():
    t = jnp.arange(DEC_SEQ)
    posn = jnp.stack([t // GRID_W, t % GRID_W], axis=-1).astype(F32)
    inv = ROPE_THETA ** (-jnp.arange(ROPE_F, dtype=F32) / ROPE_F)
    ang = posn[:, :, None] * inv
    cos, sin = jnp.cos(ang), jnp.sin(ang)
    zero = jnp.zeros_like(sin[:, 0])
    cos_t = jnp.concatenate([cos[:, 0], cos[:, 0], cos[:, 1], cos[:, 1]], axis=-1)
    sa_t = jnp.concatenate([-sin[:, 0], zero, -sin[:, 1], zero], axis=-1)
    sb_t = jnp.concatenate([zero, sin[:, 0], zero, sin[:, 1]], axis=-1)
    return cos_t, sa_t, sb_t


def kernel(x_prompt, x_sample, cache_k, cache_v, state_ssd, c, c_ctx, w_mod, b_mod, w_in, q_norm, k_norm, conv_w, conv_b, a_log, dt_bias, d_skip, ssd_norm, w_attn_o, w_ssd_o, w_out, ln1_g, ln1_b, w_mlp1, b_mlp1, w_mlp2, b_mlp2, ln2_g, ln2_b):
    l = 0
    perm = _dt_perm()
    w = w_in[l]
    o_q, o_k, o_v, o_z = 0, Q_DIM, Q_DIM + KV_DIM, Q_DIM + 2 * KV_DIM
    o_xbc = o_z + D_INNER
    o_dt = o_xbc + XBC_DIM
    o_gate = o_dt + DT_DIM
    w_dt = jnp.concatenate([jnp.take(w[:, o_dt:o_dt + DT_DIM], perm, axis=1),
                            jnp.zeros((D_MODEL, LANE - DT_DIM), F32)], axis=1).astype(BF16)
    w_main = jnp.concatenate(
        [w[:, o_z:o_z + D_INNER], w[:, o_gate:o_gate + 2 * D_MODEL], w[:, o_xbc:o_xbc + XBC_DIM],
         w[:, o_q:o_q + Q_DIM], w[:, o_k:o_k + KV_DIM], w[:, o_v:o_v + KV_DIM],
         jnp.zeros((D_MODEL, PROJ_W - COL_DT), F32)], axis=1).astype(BF16)
    pad = jnp.zeros((LANE - DT_DIM,), F32)
    bias_col = jnp.concatenate([jnp.take(dt_bias[l].reshape(-1), perm), pad]).reshape(LANE, 1)
    alog_col = jnp.concatenate([jnp.take(a_log[l].reshape(-1), perm), pad]).reshape(LANE, 1)
    dskip_row = jnp.repeat(d_skip[l], SSD_P).reshape(1, D_INNER)
    row = lambda v: v.reshape(1, -1)

    c_all = jnp.concatenate([c_ctx[None, :], c, jnp.zeros((8 - 1 - DEC_BATCH, D_MODEL), F32)], axis=0)
    mod = _modulation(c_all, w_mod[l], row(b_mod[l])).reshape(8, 6, D_MODEL)

    xp = x_prompt.reshape(N_CTX, D_MODEL)
    xl = x_sample.reshape(N_LAT, D_MODEL)
    proj, dt_raw = _inproj(xp, xl, mod, w_main, w_dt)
    dt_t, acs_t = _dtprep(dt_raw, bias_col, alog_col)

    qn, kn = row(q_norm[l]), row(k_norm[l])
    attn, k_new, v_new = _attn_ctx(proj, qn, kn)
    cos_t, sa_t, sb_t = _rope_tables()
    attn = _attn_lat(proj, cache_k[:, l].reshape(DEC_BATCH, PAST_LEN, KV_DIM),
                     cache_v[:, l].reshape(DEC_BATCH, PAST_LEN, KV_DIM), cos_t, sa_t, sb_t, qn, kn, attn)

    cw, cb = conv_w[l], row(conv_b[l])
    yssd, h_new = _ssd(proj, dt_t, acs_t, cw, cb, dskip_row, None, None, seq=SEQ, nbatch=BATCH,
                       row_blk0=0, emit_state=True)
    h0 = state_ssd[:, l].reshape(DEC_BATCH, 2, H_SSD * SSD_P, SSD_N)
    (yssd,) = _ssd(proj, dt_t, acs_t, cw, cb, dskip_row, h0, yssd, seq=DEC_SEQ, nbatch=DEC_BATCH,
                   row_blk0=N_CTX // DEC_SEQ, emit_state=False)

    x1 = _merge(attn, yssd, proj, xp, xl, mod, row(ssd_norm[l]), w_attn_o[l].astype(BF16),
                w_ssd_o[l].astype(BF16), w_out[l].astype(BF16), row(ln1_g[l]), row(ln1_b[l]))
    out_p, out_l = _mlp(x1, mod, w_mlp1[l].astype(BF16), row(b_mlp1[l]), w_mlp2[l].astype(BF16),
                        row(b_mlp2[l]), row(ln2_g[l]), row(ln2_b[l]))

    y_p = out_p.reshape(BATCH, SEQ, D_MODEL)
    y_s = out_l.reshape(DEC_BATCH, DEC_SEQ, D_MODEL)
    new_cache_k = k_new.reshape(BATCH, DEPTH, SEQ, H_KV, HD)
    new_cache_v = v_new.reshape(BATCH, DEPTH, SEQ, H_KV, HD)
    new_state = h_new.reshape(BATCH, DEPTH, 2, H_SSD, SSD_P, SSD_N)
    return (y_p, y_s, new_cache_k, new_cache_v, new_state)
```

```python
import functools

import jax
import jax.numpy as jnp
import numpy as np
from jax import lax
from jax.experimental import pallas as pl
from jax.experimental.pallas import tpu as pltpu

D_MODEL = 1024
BATCH = 16
SEQ = 256
DEC_BATCH = 4
DEC_SEQ = 1024
PAST_LEN = 512
GRID_W = 64
H_Q = 8
H_KV = 2
HD = 128
ROPE_F = HD // 4
ROPE_THETA = 10000.0
D_INNER = 2 * D_MODEL
SSD_P = 64
H_SSD = D_INNER // SSD_P
SSD_G = 8
SSD_R = H_SSD // SSD_G
SSD_N = 128
CONV_K = 5
CHUNK = 128
D_FF = 4 * D_MODEL
DEPTH = 1
ALPHA = (2.0 * DEPTH) ** 0.25
EPS = 1e-6
Q_DIM = H_Q * HD
KV_DIM = H_KV * HD
BC_DIM = SSD_G * SSD_N
XBC_DIM = D_INNER + 2 * BC_DIM
DT_DIM = 2 * H_SSD
REP = H_Q // H_KV

N_CTX = BATCH * SEQ
N_LAT = DEC_BATCH * DEC_SEQ
N_TOK = N_CTX + N_LAT

LANE = 128

COL_Z = 0
COL_GATE = COL_Z + D_INNER
COL_X = COL_GATE + 2 * D_MODEL
COL_B = COL_X + D_INNER
COL_C = COL_B + BC_DIM
COL_Q = COL_C + BC_DIM
COL_K = COL_Q + Q_DIM
COL_V = COL_K + KV_DIM
COL_DT = COL_V + KV_DIM
PROJ_W = COL_DT + LANE

VMEM_LIMIT = 52 * 1024 * 1024
MERGE_CHAIN_ROWS = 256
ATTN_HEADS_PER_CHAIN = 1

F32 = jnp.float32
BF16 = jnp.bfloat16


def _ln(x):
    mu = jnp.mean(x, axis=-1, keepdims=True)
    xc = x - mu
    var = jnp.mean(xc * xc, axis=-1, keepdims=True)
    return xc * lax.rsqrt(var + EPS)


def _rms(x, g):
    return x * lax.rsqrt(jnp.mean(x * x, axis=-1, keepdims=True) + EPS) * g


def _mod_row(i, tm):
    nct = N_CTX // tm
    tpl = DEC_SEQ // tm
    return jnp.where(i < nct, 0, 1 + jnp.maximum(i - nct, 0) // tpl)


def _ctx_blk(i, tm):
    return jnp.minimum(i, N_CTX // tm - 1)


def _lat_blk(i, tm):
    return jnp.maximum(i - N_CTX // tm, 0)


def _params(sem):
    return pltpu.CompilerParams(dimension_semantics=sem, vmem_limit_bytes=VMEM_LIMIT)


def _mod_kernel(c_ref, w_ref, b_ref, o_ref):
    c = c_ref[...]
    s = c * jax.nn.sigmoid(c)
    o_ref[...] = jnp.dot(s.astype(BF16), w_ref[...].astype(BF16), preferred_element_type=F32) + b_ref[...]


def _modulation(c_all, w_mod, b_mod):
    tn = 1536
    return pl.pallas_call(
        _mod_kernel,
        grid=(6 * D_MODEL // tn,),
        in_specs=[pl.BlockSpec((8, D_MODEL), lambda j: (0, 0)),
                  pl.BlockSpec((D_MODEL, tn), lambda j: (0, j)),
                  pl.BlockSpec((1, tn), lambda j: (0, j))],
        out_specs=pl.BlockSpec((8, tn), lambda j: (0, j)),
        out_shape=jax.ShapeDtypeStruct((8, 6 * D_MODEL), F32),
        compiler_params=_params(("arbitrary",)),
        name="modulation",
    )(c_all, w_mod, b_mod)


def _inproj_kernel(xp_ref, xl_ref, mod_ref, w_ref, wdt_ref, o_ref, dt_ref, h_scr, *, tm):
    @pl.when(pl.program_id(1) == 0)
    def _():
        x = jnp.where(pl.program_id(0) < N_CTX // tm, xp_ref[...], xl_ref[...])
        h = _ln(x) * (1.0 + mod_ref[1:2, :]) + mod_ref[0:1, :]
        h_scr[...] = h.astype(BF16)
        dt_ref[...] = jnp.dot(h_scr[...], wdt_ref[...], preferred_element_type=F32)

    o_ref[...] = jnp.dot(h_scr[...], w_ref[...], preferred_element_type=F32).astype(BF16)


def _inproj(xp, xl, mod, w_main, w_dt):
    tm, tn = 1024, 1408
    return pl.pallas_call(
        functools.partial(_inproj_kernel, tm=tm),
        grid=(N_TOK // tm, PROJ_W // tn),
        in_specs=[pl.BlockSpec((tm, D_MODEL), lambda i, j: (_ctx_blk(i, tm), 0)),
                  pl.BlockSpec((tm, D_MODEL), lambda i, j: (_lat_blk(i, tm), 0)),
                  pl.BlockSpec((None, 6, D_MODEL), lambda i, j: (_mod_row(i, tm), 0, 0)),
                  pl.BlockSpec((D_MODEL, tn), lambda i, j: (0, j)),
                  pl.BlockSpec((D_MODEL, LANE), lambda i, j: (0, 0))],
        out_specs=[pl.BlockSpec((tm, tn), lambda i, j: (i, j)),
                   pl.BlockSpec((tm, LANE), lambda i, j: (i, 0))],
        out_shape=[jax.ShapeDtypeStruct((N_TOK, PROJ_W), BF16),
                   jax.ShapeDtypeStruct((N_TOK, LANE), F32)],
        scratch_shapes=[pltpu.VMEM((tm, D_MODEL), BF16)],
        compiler_params=_params(("arbitrary", "arbitrary")),
        name="inproj",
    )(xp, xl, mod, w_main, w_dt)


def _dtprep_kernel(p_ref, bias_ref, alog_ref, dt_ref, acs_ref, *, nchunk):
    a_head = -jnp.exp(alog_ref[...])
    ii = lax.broadcasted_iota(jnp.int32, (CHUNK, CHUNK), 0)
    jj = lax.broadcasted_iota(jnp.int32, (CHUNK, CHUNK), 1)
    upper = (ii <= jj).astype(F32)
    lower = (ii >= jj).astype(F32)
    is_fwd = (ii & 7) < SSD_R
    for c in range(nchunk):
        t = p_ref[c * CHUNK:(c + 1) * CHUNK, :]
        tt = t.T + bias_ref[...]
        dtv = jnp.maximum(tt, 0.0) + jnp.log1p(jnp.exp(-jnp.abs(tt)))
        a = dtv * a_head
        pre = jnp.dot(a, upper, precision=lax.Precision.HIGHEST, preferred_element_type=F32)
        suf = jnp.dot(a, lower, precision=lax.Precision.HIGHEST, preferred_element_type=F32)
        dt_ref[c] = dtv
        acs_ref[c] = jnp.where(is_fwd, pre, suf)


def _dtprep(proj, bias_col, alog_col):
    nchunk = 4
    tm = nchunk * CHUNK
    nct = N_TOK // CHUNK
    out = jax.ShapeDtypeStruct((nct, LANE, CHUNK), F32)
    return pl.pallas_call(
        functools.partial(_dtprep_kernel, nchunk=nchunk),
        grid=(N_TOK // tm,),
        in_specs=[pl.BlockSpec((tm, LANE), lambda i: (i, 0)),
                  pl.BlockSpec((LANE, 1), lambda i: (0, 0)),
                  pl.BlockSpec((LANE, 1), lambda i: (0, 0))],
        out_specs=[pl.BlockSpec((nchunk, LANE, CHUNK), lambda i: (i, 0, 0)),
                   pl.BlockSpec((nchunk, LANE, CHUNK), lambda i: (i, 0, 0))],
        out_shape=[out, out],
        compiler_params=_params(("arbitrary",)),
        name="dtprep",
    )(proj, bias_col, alog_col)


def _rope(x, cos, sa, sb):
    return x * cos + pltpu.roll(x, HD - ROPE_F, axis=1) * sa + pltpu.roll(x, ROPE_F, axis=1) * sb


def _softmax_pv(q4, k_b, v_b):
    s = lax.dot_general(q4.astype(BF16), k_b, (((1,), (1,)), ((), ())), preferred_element_type=F32)
    m = jnp.max(s, axis=-1, keepdims=True)
    p = jnp.exp(s - m)
    l = jnp.sum(p, axis=-1, keepdims=True)
    o = jnp.dot(p.astype(BF16), v_b, preferred_element_type=F32)
    return o / l


def _attn_ctx_kernel(q_ref, k_ref, v_ref, qn_ref, kn_ref, o_ref, ko_ref, vo_ref):
    k = _rms(k_ref[...].astype(F32), kn_ref[...])
    v = v_ref[...]
    ko_ref[...] = k
    vo_ref[...] = v.astype(F32)
    q = q_ref[...].astype(F32)
    qg = qn_ref[...] * (HD ** -0.5)
    q4 = jnp.concatenate([_rms(q[:, h * HD:(h + 1) * HD], qg) for h in range(REP)], axis=0)
    o = _softmax_pv(q4, k.astype(BF16), v)
    for h in range(REP):
        o_ref[:, h * HD:(h + 1) * HD] = o[h * SEQ:(h + 1) * SEQ].astype(BF16)


def _attn_ctx(proj, q_norm, k_norm):
    qw = REP * HD
    kv_out = jax.ShapeDtypeStruct((BATCH, SEQ, KV_DIM), F32)
    return pl.pallas_call(
        _attn_ctx_kernel,
        grid=(BATCH, H_KV),
        in_specs=[pl.BlockSpec((SEQ, qw), lambda b, g: (b, COL_Q // qw + g)),
                  pl.BlockSpec((SEQ, HD), lambda b, g: (b, COL_K // HD + g)),
                  pl.BlockSpec((SEQ, HD), lambda b, g: (b, COL_V // HD + g)),
                  pl.BlockSpec((1, HD), lambda b, g: (0, 0)),
                  pl.BlockSpec((1, HD), lambda b, g: (0, 0))],
        out_specs=[pl.BlockSpec((SEQ, qw), lambda b, g: (b, g)),
                   pl.BlockSpec((None, SEQ, HD), lambda b, g: (b, 0, g)),
                   pl.BlockSpec((None, SEQ, HD), lambda b, g: (b, 0, g))],
        out_shape=[jax.ShapeDtypeStruct((N_TOK, Q_DIM), BF16), kv_out, kv_out],
        compiler_params=_params(("arbitrary", "arbitrary")),
        name="attn_ctx",
    )(proj, proj, proj, q_norm, k_norm)


def _attn_lat_kernel(q_ref, k_ref, v_ref, ck_ref, cv_ref, cos_ref, sa_ref, sb_ref, qn_ref, kn_ref, prev_ref,
                     o_ref, kall, vall, *, tq):
    del prev_ref
    qi = pl.program_id(2)

    @pl.when(qi == 0)
    def _():
        k = _rope(_rms(k_ref[...].astype(F32), kn_ref[...]), cos_ref[...], sa_ref[...], sb_ref[...])
        kall[0:PAST_LEN, :] = ck_ref[...].astype(BF16)
        kall[PAST_LEN:, :] = k.astype(BF16)
        vall[0:PAST_LEN, :] = cv_ref[...].astype(BF16)
        vall[PAST_LEN:, :] = v_ref[...]

    r0 = pl.multiple_of(qi * tq, tq)
    cos = cos_ref[pl.ds(r0, tq), :]
    sa = sa_ref[pl.ds(r0, tq), :]
    sb = sb_ref[pl.ds(r0, tq), :]
    qg = qn_ref[...] * (HD ** -0.5)
    for h0 in range(0, REP, ATTN_HEADS_PER_CHAIN):
        hs = range(h0, h0 + ATTN_HEADS_PER_CHAIN)
        qh = jnp.concatenate(
            [_rope(_rms(q_ref[:, h * HD:(h + 1) * HD].astype(F32), qg), cos, sa, sb) for h in hs], axis=0)
        o = _softmax_pv(qh, kall[...], vall[...])
        for n, h in enumerate(hs):
            o_ref[:, h * HD:(h + 1) * HD] = o[n * tq:(n + 1) * tq].astype(BF16)


def _attn_lat(proj, cache_k, cache_v, cos, sa, sb, q_norm, k_norm, attn_prev):
    tq = 256
    qw = REP * HD
    nq = DEC_SEQ // tq
    row0 = N_CTX // tq
    tab = pl.BlockSpec((DEC_SEQ, HD), lambda b, g, i: (0, 0))
    return pl.pallas_call(
        functools.partial(_attn_lat_kernel, tq=tq),
        grid=(DEC_BATCH, H_KV, nq),
        in_specs=[pl.BlockSpec((tq, qw), lambda b, g, i: (row0 + b * nq + i, COL_Q // qw + g)),
                  pl.BlockSpec((DEC_SEQ, HD), lambda b, g, i: (N_CTX // DEC_SEQ + b, COL_K // HD + g)),
                  pl.BlockSpec((DEC_SEQ, HD), lambda b, g, i: (N_CTX // DEC_SEQ + b, COL_V // HD + g)),
                  pl.BlockSpec((None, PAST_LEN, HD), lambda b, g, i: (b, 0, g)),
                  pl.BlockSpec((None, PAST_LEN, HD), lambda b, g, i: (b, 0, g)),
                  tab, tab, tab,
                  pl.BlockSpec((1, HD), lambda b, g, i: (0, 0)),
                  pl.BlockSpec((1, HD), lambda b, g, i: (0, 0)),
                  pl.BlockSpec(memory_space=pl.ANY)],
        out_specs=pl.BlockSpec((tq, qw), lambda b, g, i: (row0 + b * nq + i, g)),
        out_shape=jax.ShapeDtypeStruct((N_TOK, Q_DIM), BF16),
        input_output_aliases={10: 0},
        scratch_shapes=[pltpu.VMEM((PAST_LEN + DEC_SEQ, HD), BF16),
                        pltpu.VMEM((PAST_LEN + DEC_SEQ, HD), BF16)],
        compiler_params=_params(("arbitrary", "arbitrary", "arbitrary")),
        name="attn_lat",
    )(proj, proj, proj, cache_k, cache_v, cos, sa, sb, q_norm, k_norm, attn_prev)


def _ssd_kernel(*refs, seq, has_h0, emit_state, has_prev):
    (x_ref, b_ref, c_ref, dt_ref, acs_ref, cwx_ref, cwb_ref, cwc_ref,
     cbx_ref, cbb_ref, cbc_ref, dsk_ref) = refs[:12]
    pos = 12
    h0_ref = None
    if has_h0:
        h0_ref = refs[pos]
        pos += 1
    if has_prev:
        pos += 1
    y_ref = refs[pos]
    pos += 1
    hout_ref = None
    if emit_state:
        hout_ref = refs[pos]
        pos += 1
    pad, xst, btf, cbf, csb, acc, ht, yacc = refs[pos:]
    nc = seq // CHUNK
    half = SSD_R * SSD_P // 2
    halo = 8

    ii = lax.broadcasted_iota(jnp.int32, (CHUNK, CHUNK), 0)
    jj = lax.broadcasted_iota(jnp.int32, (CHUNK, CHUNK), 1)
    lane_lo = jj < SSD_P

    def pair(a, b):
        return jnp.where(lane_lo, a, b)

    pad[0:halo, :] = jnp.zeros((halo, 4 * LANE), F32)
    pad[halo + seq:, :] = jnp.zeros((halo, 4 * LANE), F32)
    pad[halo:halo + seq, 0:2 * LANE] = x_ref[...].astype(F32)
    pad[halo:halo + seq, 2 * LANE:3 * LANE] = b_ref[...].astype(F32)
    pad[halo:halo + seq, 3 * LANE:] = c_ref[...].astype(F32)

    segs = [(cwx_ref, cbx_ref, 0), (cwx_ref, cbx_ref, LANE), (cwb_ref, cbb_ref, 0), (cwc_ref, cbc_ref, 0)]
    for rb in range(nc):
        vals = []
        for blk, (w_ref, bias_ref, lo) in enumerate(segs):
            a = None
            for k in range(CONV_K):
                s = halo + rb * CHUNK + k - CONV_K // 2
                term = pad[s:s + CHUNK, blk * LANE:(blk + 1) * LANE] * w_ref[k:k + 1, lo:lo + LANE]
                a = term if a is None else a + term
            a = a + bias_ref[:, lo:lo + LANE]
            vals.append(a * jax.nn.sigmoid(a))
        for p in range(2):
            v = vals[p]
            yacc[rb * CHUNK:(rb + 1) * CHUNK, p * LANE:(p + 1) * LANE] = v * dsk_ref[:, p * LANE:(p + 1) * LANE]
            xst[rb, p, 0:CHUNK, :] = jnp.where(lane_lo, v, 0.0).astype(BF16)
            xst[rb, p, CHUNK:, :] = jnp.where(lane_lo, 0.0, v).astype(BF16)
        vb = vals[2]
        vc = vals[3].astype(BF16)
        btf[rb] = vb.T
        csb[rb] = vc
        cbf[rb] = lax.dot_general(vc, vb.astype(BF16), (((1,), (1,)), ((), ())), preferred_element_type=F32)
        acc[rb] = jnp.concatenate([acs_ref[rb]] * (CHUNK // 8), axis=0).T

    for d in range(2):
        if has_h0:
            ht[d, :, 0:half] = h0_ref[d, 0:half, :].T
            ht[d, :, half:] = h0_ref[d, half:, :].T
        else:
            ht[d] = jnp.zeros((SSD_N, 2 * half), F32)

    def chunk(c, d):
        r0 = pl.multiple_of(c * CHUNK, CHUNK)
        cbm = cbf[c]
        bt = btf[c]
        cc = csb[c]
        acol = acc[c]
        arow = acs_ref[c]
        drow = dt_ref[c]
        h_prev = ht[d]
        yoff = jnp.dot(cc, h_prev.astype(BF16), preferred_element_type=F32)
        mask = (ii >= jj) if d == 0 else (ii <= jj)
        cbm = jnp.where(mask, cbm, 0.0)
        last = CHUNK - 1 if d == 0 else 0
        ms, bws, eas, cds = [], [], [], []
        for r in range(SSD_R):
            col = d * SSD_R + r
            acb = jnp.broadcast_to(acol[:, col:col + 1], (CHUNK, CHUNK))
            ar = arow[col:col + 1, :]
            dr = drow[col:col + 1, :]
            dec = jnp.exp(jnp.minimum(acb - ar, 0.0))
            ms.append((cbm * dec * dr).astype(BF16))
            eas.append(jnp.exp(acb))
            a_last = ar[:, last:last + 1]
            bws.append((bt * (dr * jnp.exp(a_last - ar))).astype(BF16))
            cds.append(jnp.exp(a_last))
        ys, sts, cdm = [], [], []
        for p in range(2):
            r_a, r_b = 2 * p, 2 * p + 1
            x2 = xst[c, p]
            yd = jnp.dot(jnp.concatenate([ms[r_a], ms[r_b]], axis=1), x2, preferred_element_type=F32)
            ys.append(yd + yoff[:, p * LANE:(p + 1) * LANE] * pair(eas[r_a], eas[r_b]))
            sts.append(jnp.dot(jnp.concatenate([bws[r_a], bws[r_b]], axis=1), x2, preferred_element_type=F32))
            cdm.append(pair(cds[r_a], cds[r_b]))
        ht[d] = h_prev * jnp.concatenate(cdm, axis=1) + jnp.concatenate(sts, axis=1)
        yacc[pl.ds(r0, CHUNK), :] += jnp.concatenate(ys, axis=1)

    def body(t, carry):
        chunk(t, 0)
        chunk(nc - 1 - t, 1)
        return carry

    lax.fori_loop(0, nc, body, 0)
    y_ref[...] = yacc[...].astype(BF16)

    if emit_state:
        for d in range(2):
            hout_ref[d, 0:half, :] = ht[d, :, 0:half].T
            hout_ref[d, half:, :] = ht[d, :, half:].T


def _ssd(proj, dt_t, acs_t, conv_w, conv_b, dskip, h0, y_prev, *, seq, nbatch, row_blk0, emit_state):
    gw = SSD_R * SSD_P
    nc = seq // CHUNK
    has_h0, has_prev = h0 is not None, y_prev is not None
    xb0, bb0, cb0 = COL_X // gw, COL_B // SSD_N, COL_C // SSD_N
    in_specs = [
        pl.BlockSpec((seq, gw), lambda b, g: (row_blk0 + b, xb0 + g)),
        pl.BlockSpec((seq, SSD_N), lambda b, g: (row_blk0 + b, bb0 + g)),
        pl.BlockSpec((seq, SSD_N), lambda b, g: (row_blk0 + b, cb0 + g)),
        pl.BlockSpec((nc, 8, CHUNK), lambda b, g: (row_blk0 + b, g, 0)),
        pl.BlockSpec((nc, 8, CHUNK), lambda b, g: (row_blk0 + b, g, 0)),
        pl.BlockSpec((CONV_K, gw), lambda b, g: (0, g)),
        pl.BlockSpec((CONV_K, SSD_N), lambda b, g: (0, D_INNER // SSD_N + g)),
        pl.BlockSpec((CONV_K, SSD_N), lambda b, g: (0, (D_INNER + BC_DIM) // SSD_N + g)),
        pl.BlockSpec((1, gw), lambda b, g: (0, g)),
        pl.BlockSpec((1, SSD_N), lambda b, g: (0, D_INNER // SSD_N + g)),
        pl.BlockSpec((1, SSD_N), lambda b, g: (0, (D_INNER + BC_DIM) // SSD_N + g)),
        pl.BlockSpec((1, gw), lambda b, g: (0, g)),
    ]
    args = [proj, proj, proj, dt_t, acs_t, conv_w, conv_w, conv_w, conv_b, conv_b, conv_b, dskip]
    state_spec = pl.BlockSpec((None, 2, gw, SSD_N), lambda b, g: (b, 0, g, 0))
    aliases = {}
    if has_h0:
        in_specs.append(state_spec)
        args.append(h0)
    if has_prev:
        aliases = {len(args): 0}
        in_specs.append(pl.BlockSpec(memory_space=pl.ANY))
        args.append(y_prev)
    out_specs = [pl.BlockSpec((seq, gw), lambda b, g: (row_blk0 + b, g))]
    out_shape = [jax.ShapeDtypeStruct((N_TOK, D_INNER), BF16)]
    if emit_state:
        out_specs.append(state_spec)
        out_shape.append(jax.ShapeDtypeStruct((nbatch, 2, H_SSD * SSD_P, SSD_N), F32))
    return pl.pallas_call(
        functools.partial(_ssd_kernel, seq=seq, has_h0=has_h0, emit_state=emit_state, has_prev=has_prev),
        grid=(nbatch, SSD_G),
        in_specs=in_specs,
        out_specs=out_specs,
        out_shape=out_shape,
        input_output_aliases=aliases,
        scratch_shapes=[pltpu.VMEM((seq + 16, 4 * LANE), F32),
                        pltpu.VMEM((nc, 2, 2 * CHUNK, LANE), BF16),
                        pltpu.VMEM((nc, SSD_N, CHUNK), F32),
                        pltpu.VMEM((nc, CHUNK, CHUNK), F32),
                        pltpu.VMEM((nc, CHUNK, SSD_N), BF16),
                        pltpu.VMEM((nc, CHUNK, LANE), F32),
                        pltpu.VMEM((2, SSD_N, gw), F32),
                        pltpu.VMEM((seq, gw), F32)],
        compiler_params=_params(("arbitrary", "arbitrary")),
        name="ssd_lat" if has_h0 else "ssd_ctx",
    )(*args)


def _merge_kernel(attn_ref, y_ref, z_ref, g_ref, xp_ref, xl_ref, mod_ref, nrm_ref, wa_ref, ws_ref, wo_ref,
                  l1g_ref, l1b_ref, o_ref, *, tm):
    is_ctx = pl.program_id(0) < N_CTX // tm
    for r0 in range(0, tm, MERGE_CHAIN_ROWS):
        rows = slice(r0, r0 + MERGE_CHAIN_ROWS)
        z = z_ref[rows, :].astype(F32)
        yg = y_ref[rows, :].astype(F32) * (z * jax.nn.sigmoid(z))
        yn = yg * lax.rsqrt(jnp.mean(yg * yg, axis=-1, keepdims=True) + EPS) * nrm_ref[...]
        pa = jnp.dot(attn_ref[rows, :], wa_ref[...], preferred_element_type=F32)
        ps = jnp.dot(yn.astype(BF16), ws_ref[...], preferred_element_type=F32)
        g = g_ref[rows, :].astype(F32)
        merged = jax.nn.sigmoid(g[:, :D_MODEL]) * pa + jax.nn.sigmoid(g[:, D_MODEL:]) * ps
        mix = jnp.dot(merged.astype(BF16), wo_ref[...], preferred_element_type=F32)
        x = jnp.where(is_ctx, xp_ref[rows, :], xl_ref[rows, :])
        x1 = _ln(ALPHA * x + mod_ref[2:3, :] * mix)
        o_ref[rows, :] = x1 * l1g_ref[...] + l1b_ref[...]


def _merge(attn, yssd, proj, xp, xl, mod, ssd_norm, wa, ws, wo, l1g, l1b):
    tm = 256
    const = lambda shape: pl.BlockSpec(shape, lambda i: (0, 0))
    return pl.pallas_call(
        functools.partial(_merge_kernel, tm=tm),
        grid=(N_TOK // tm,),
        in_specs=[pl.BlockSpec((tm, Q_DIM), lambda i: (i, 0)),
                  pl.BlockSpec((tm, D_INNER), lambda i: (i, 0)),
                  pl.BlockSpec((tm, D_INNER), lambda i: (i, COL_Z // D_INNER)),
                  pl.BlockSpec((tm, 2 * D_MODEL), lambda i: (i, COL_GATE // (2 * D_MODEL))),
                  pl.BlockSpec((tm, D_MODEL), lambda i: (_ctx_blk(i, tm), 0)),
                  pl.BlockSpec((tm, D_MODEL), lambda i: (_lat_blk(i, tm), 0)),
                  pl.BlockSpec((None, 6, D_MODEL), lambda i: (_mod_row(i, tm), 0, 0)),
                  const((1, D_INNER)), const((Q_DIM, D_MODEL)), const((D_INNER, D_MODEL)),
                  const((D_MODEL, D_MODEL)), const((1, D_MODEL)), const((1, D_MODEL))],
        out_specs=pl.BlockSpec((tm, D_MODEL), lambda i: (i, 0)),
        out_shape=jax.ShapeDtypeStruct((N_TOK, D_MODEL), F32),
        compiler_params=_params(("arbitrary",)),
        name="merge",
    )(attn, yssd, proj, proj, xp, xl, mod, ssd_norm, wa, ws, wo, l1g, l1b)


def _mlp_kernel(x1_ref, mod_ref, w1_ref, b1_ref, w2_ref, b2_ref, l2g_ref, l2b_ref, op_ref, ol_ref,
                h_scr, acc_scr, *, tm):
    i = pl.program_id(0)
    j = pl.program_id(1)

    @pl.when(j == 0)
    def _():
        h = _ln(x1_ref[...]) * (1.0 + mod_ref[4:5, :]) + mod_ref[3:4, :]
        h_scr[...] = h.astype(BF16)
        acc_scr[...] = jnp.zeros_like(acc_scr)

    u = jnp.dot(h_scr[...], w1_ref[...], preferred_element_type=F32) + b1_ref[...]
    u = jnp.maximum(u, 0.0)
    acc_scr[...] += jnp.dot((u * u).astype(BF16), w2_ref[...], preferred_element_type=F32)

    def result():
        ff = acc_scr[...] + b2_ref[...]
        x2 = _ln(ALPHA * x1_ref[...] + mod_ref[5:6, :] * ff)
        return x2 * l2g_ref[...] + l2b_ref[...]

    last = j == pl.num_programs(1) - 1
    is_ctx = i < N_CTX // tm

    @pl.when(last & is_ctx)
    def _():
        op_ref[...] = result()

    @pl.when(last & jnp.logical_not(is_ctx))
    def _():
        ol_ref[...] = result()


def _mlp(x1, mod, w1, b1, w2, b2, l2g, l2b):
    tm, tf = 1024, 512
    return pl.pallas_call(
        functools.partial(_mlp_kernel, tm=tm),
        grid=(N_TOK // tm, D_FF // tf),
        in_specs=[pl.BlockSpec((tm, D_MODEL), lambda i, j: (i, 0)),
                  pl.BlockSpec((None, 6, D_MODEL), lambda i, j: (_mod_row(i, tm), 0, 0)),
                  pl.BlockSpec((D_MODEL, tf), lambda i, j: (0, j)),
                  pl.BlockSpec((1, tf), lambda i, j: (0, j)),
                  pl.BlockSpec((tf, D_MODEL), lambda i, j: (j, 0)),
                  pl.BlockSpec((1, D_MODEL), lambda i, j: (0, 0)),
                  pl.BlockSpec((1, D_MODEL), lambda i, j: (0, 0)),
                  pl.BlockSpec((1, D_MODEL), lambda i, j: (0, 0))],
        out_specs=[pl.BlockSpec((tm, D_MODEL), lambda i, j: (_ctx_blk(i, tm), 0)),
                   pl.BlockSpec((tm, D_MODEL), lambda i, j: (_lat_blk(i, tm), 0))],
        out_shape=[jax.ShapeDtypeStruct((N_CTX, D_MODEL), F32), jax.ShapeDtypeStruct((N_LAT, D_MODEL), F32)],
        scratch_shapes=[pltpu.VMEM((tm, D_MODEL), BF16), pltpu.VMEM((tm, D_MODEL), F32)],
        compiler_params=_params(("arbitrary", "arbitrary")),
        name="mlp",
    )(x1, mod, w1, b1, w2, b2, l2g, l2b)


def _dt_perm():
    idx = []
    for n in range(DT_DIM):
        g, d, r = n // 8, (n % 8) // SSD_R, n % SSD_R
        idx.append(d * H_SSD + g * SSD_R + r)
    return jnp.asarray(idx, dtype=jnp.int32)


def _rope_tables():
    t = np.arange(DEC_SEQ)
    posn = np.stack([t // GRID_W, t % GRID_W], axis=-1).astype(np.float32)
    inv = (ROPE_THETA ** (-np.arange(ROPE_F, dtype=np.float32) / ROPE_F)).astype(np.float32)
    ang = posn[:, :, None] * inv
    cos, sin = np.cos(ang), np.sin(ang)
    zero = np.zeros_like(sin[:, 0])
    cos_t = np.concatenate([cos[:, 0], cos[:, 0], cos[:, 1], cos[:, 1]], axis=-1)
    sa_t = np.concatenate([-sin[:, 0], zero, -sin[:, 1], zero], axis=-1)
    sb_t = np.concatenate([zero, sin[:, 0], zero, sin[:, 1]], axis=-1)
    return tuple(jnp.asarray(a, dtype=F32) for a in (cos_t, sa_t, sb_t))


def kernel(x_prompt, x_sample, cache_k, cache_v, state_ssd, c, c_ctx, w_mod, b_mod, w_in, q_norm, k_norm, conv_w, conv_b, a_log, dt_bias, d_skip, ssd_norm, w_attn_o, w_ssd_o, w_out, ln1_g, ln1_b, w_mlp1, b_mlp1, w_mlp2, b_mlp2, ln2_g, ln2_b):
    l = 0
    perm = _dt_perm()
    w = w_in[l]
    o_q, o_k, o_v, o_z = 0, Q_DIM, Q_DIM + KV_DIM, Q_DIM + 2 * KV_DIM
    o_xbc = o_z + D_INNER
    o_dt = o_xbc + XBC_DIM
    o_gate = o_dt + DT_DIM
    w_dt = jnp.concatenate([jnp.take(w[:, o_dt:o_dt + DT_DIM], perm, axis=1),
                            jnp.zeros((D_MODEL, LANE - DT_DIM), F32)], axis=1).astype(BF16)
    w_main = jnp.concatenate(
        [w[:, o_z:o_z + D_INNER], w[:, o_gate:o_gate + 2 * D_MODEL], w[:, o_xbc:o_xbc + XBC_DIM],
         w[:, o_q:o_q + Q_DIM], w[:, o_k:o_k + KV_DIM], w[:, o_v:o_v + KV_DIM],
         jnp.zeros((D_MODEL, PROJ_W - COL_DT), F32)], axis=1).astype(BF16)
    pad = jnp.zeros((LANE - DT_DIM,), F32)
    bias_col = jnp.concatenate([jnp.take(dt_bias[l].reshape(-1), perm), pad]).reshape(LANE, 1)
    alog_col = jnp.concatenate([jnp.take(a_log[l].reshape(-1), perm), pad]).reshape(LANE, 1)
    dskip_row = jnp.repeat(d_skip[l], SSD_P).reshape(1, D_INNER)
    row = lambda v: v.reshape(1, -1)

    c_all = jnp.concatenate([c_ctx[None, :], c, jnp.zeros((8 - 1 - DEC_BATCH, D_MODEL), F32)], axis=0)
    mod = _modulation(c_all, w_mod[l], row(b_mod[l])).reshape(8, 6, D_MODEL)

    xp = x_prompt.reshape(N_CTX, D_MODEL)
    xl = x_sample.reshape(N_LAT, D_MODEL)
    proj, dt_raw = _inproj(xp, xl, mod, w_main, w_dt)
    dt_t, acs_t = _dtprep(dt_raw, bias_col, alog_col)

    qn, kn = row(q_norm[l]), row(k_norm[l])
    attn, k_new, v_new = _attn_ctx(proj, qn, kn)
    cos_t, sa_t, sb_t = _rope_tables()
    attn = _attn_lat(proj, cache_k[:, l].reshape(DEC_BATCH, PAST_LEN, KV_DIM),
                     cache_v[:, l].reshape(DEC_BATCH, PAST_LEN, KV_DIM), cos_t, sa_t, sb_t, qn, kn, attn)

    cw, cb = conv_w[l], row(conv_b[l])
    yssd, h_new = _ssd(proj, dt_t, acs_t, cw, cb, dskip_row, None, None, seq=SEQ, nbatch=BATCH,
                       row_blk0=0, emit_state=True)
    h0 = state_ssd[:, l].reshape(DEC_BATCH, 2, H_SSD * SSD_P, SSD_N)
    (yssd,) = _ssd(proj, dt_t, acs_t, cw, cb, dskip_row, h0, yssd, seq=DEC_SEQ, nbatch=DEC_BATCH,
                   row_blk0=N_CTX // DEC_SEQ, emit_state=False)

    x1 = _merge(attn, yssd, proj, xp, xl, mod, row(ssd_norm[l]), w_attn_o[l].astype(BF16),
                w_ssd_o[l].astype(BF16), w_out[l].astype(BF16), row(ln1_g[l]), row(ln1_b[l]))
    out_p, out_l = _mlp(x1, mod, w_mlp1[l].astype(BF16), row(b_mlp1[l]), w_mlp2[l].astype(BF16),
                        row(b_mlp2[l]), row(ln2_g[l]), row(ln2_b[l]))

    y_p = out_p.reshape(BATCH, SEQ, D_MODEL)
    y_s = out_l.reshape(DEC_BATCH, DEC_SEQ, D_MODEL)
    new_cache_k = k_new.reshape(BATCH, DEPTH, SEQ, H_KV, HD)
    new_cache_v = v_new.reshape(BATCH, DEPTH, SEQ, H_KV, HD)
    new_state = h_new.reshape(BATCH, DEPTH, 2, H_SSD, SSD_P, SSD_N)
    return (y_p, y_s, new_cache_k, new_cache_v, new_state)
```

```python
import functools

import jax
import jax.numpy as jnp
import numpy as np
from jax import lax
from jax.experimental import pallas as pl
from jax.experimental.pallas import tpu as pltpu

D_MODEL = 1024
BATCH = 16
SEQ = 256
DEC_BATCH = 4
DEC_SEQ = 1024
PAST_LEN = 512
GRID_W = 64
H_Q = 8
H_KV = 2
HD = 128
ROPE_F = HD // 4
ROPE_THETA = 10000.0
D_INNER = 2 * D_MODEL
SSD_P = 64
H_SSD = D_INNER // SSD_P
SSD_G = 8
SSD_R = H_SSD // SSD_G
SSD_N = 128
CONV_K = 5
CHUNK = 128
D_FF = 4 * D_MODEL
DEPTH = 1
ALPHA = (2.0 * DEPTH) ** 0.25
EPS = 1e-6
LOG2E = 1.4426950408889634
Q_DIM = H_Q * HD
KV_DIM = H_KV * HD
BC_DIM = SSD_G * SSD_N
XBC_DIM = D_INNER + 2 * BC_DIM
DT_DIM = 2 * H_SSD
REP = H_Q // H_KV

N_CTX = BATCH * SEQ
N_LAT = DEC_BATCH * DEC_SEQ
N_TOK = N_CTX + N_LAT

LANE = 128

COL_Z = 0
COL_GATE = COL_Z + D_INNER
COL_X = COL_GATE + 2 * D_MODEL
COL_B = COL_X + D_INNER
COL_C = COL_B + BC_DIM
COL_Q = COL_C + BC_DIM
COL_K = COL_Q + Q_DIM
COL_V = COL_K + KV_DIM
COL_DT = COL_V + KV_DIM
PROJ_W = COL_DT + LANE

VMEM_LIMIT = 52 * 1024 * 1024
MERGE_CHAIN_ROWS = 256
ATTN_HEADS_PER_CHAIN = 1

F32 = jnp.float32
BF16 = jnp.bfloat16


def _ln(x):
    mu = jnp.mean(x, axis=-1, keepdims=True)
    xc = x - mu
    var = jnp.mean(xc * xc, axis=-1, keepdims=True)
    return xc * lax.rsqrt(var + EPS)


def _rms(x, g):
    return x * lax.rsqrt(jnp.mean(x * x, axis=-1, keepdims=True) + EPS) * g


def _mod_row(i, tm):
    nct = N_CTX // tm
    tpl = DEC_SEQ // tm
    return jnp.where(i < nct, 0, 1 + jnp.maximum(i - nct, 0) // tpl)


def _ctx_blk(i, tm):
    return jnp.minimum(i, N_CTX // tm - 1)


def _lat_blk(i, tm):
    return jnp.maximum(i - N_CTX // tm, 0)


def _params(sem):
    return pltpu.CompilerParams(dimension_semantics=sem, vmem_limit_bytes=VMEM_LIMIT)


def _mod_kernel(c_ref, w_ref, b_ref, o_ref):
    c = c_ref[...]
    s = c * jax.nn.sigmoid(c)
    o_ref[...] = jnp.dot(s.astype(BF16), w_ref[...].astype(BF16), preferred_element_type=F32) + b_ref[...]


def _modulation(c_all, w_mod, b_mod):
    tn = 1536
    return pl.pallas_call(
        _mod_kernel,
        grid=(6 * D_MODEL // tn,),
        in_specs=[pl.BlockSpec((8, D_MODEL), lambda j: (0, 0)),
                  pl.BlockSpec((D_MODEL, tn), lambda j: (0, j)),
                  pl.BlockSpec((1, tn), lambda j: (0, j))],
        out_specs=pl.BlockSpec((8, tn), lambda j: (0, j)),
        out_shape=jax.ShapeDtypeStruct((8, 6 * D_MODEL), F32),
        compiler_params=_params(("arbitrary",)),
        name="modulation",
    )(c_all, w_mod, b_mod)


def _inproj_kernel(xp_ref, xl_ref, mod_ref, w_ref, wdt_ref, o_ref, dt_ref, h_scr, *, tm):
    @pl.when(pl.program_id(1) == 0)
    def _():
        x = jnp.where(pl.program_id(0) < N_CTX // tm, xp_ref[...], xl_ref[...])
        h = _ln(x) * (1.0 + mod_ref[1:2, :]) + mod_ref[0:1, :]
        h_scr[...] = h.astype(BF16)
        dt_ref[...] = jnp.dot(h_scr[...], wdt_ref[...], preferred_element_type=F32)

    o_ref[...] = jnp.dot(h_scr[...], w_ref[...], preferred_element_type=F32).astype(BF16)


def _inproj(xp, xl, mod, w_main, w_dt):
    tm, tn = 1024, 1408
    return pl.pallas_call(
        functools.partial(_inproj_kernel, tm=tm),
        grid=(N_TOK // tm, PROJ_W // tn),
        in_specs=[pl.BlockSpec((tm, D_MODEL), lambda i, j: (_ctx_blk(i, tm), 0)),
                  pl.BlockSpec((tm, D_MODEL), lambda i, j: (_lat_blk(i, tm), 0)),
                  pl.BlockSpec((None, 6, D_MODEL), lambda i, j: (_mod_row(i, tm), 0, 0)),
                  pl.BlockSpec((D_MODEL, tn), lambda i, j: (0, j)),
                  pl.BlockSpec((D_MODEL, LANE), lambda i, j: (0, 0))],
        out_specs=[pl.BlockSpec((tm, tn), lambda i, j: (i, j)),
                   pl.BlockSpec((tm, LANE), lambda i, j: (i, 0))],
        out_shape=[jax.ShapeDtypeStruct((N_TOK, PROJ_W), BF16),
                   jax.ShapeDtypeStruct((N_TOK, LANE), F32)],
        scratch_shapes=[pltpu.VMEM((tm, D_MODEL), BF16)],
        compiler_params=_params(("arbitrary", "arbitrary")),
        name="inproj",
    )(xp, xl, mod, w_main, w_dt)


def _dtprep_kernel(p_ref, bias_ref, alog_ref, dt_ref, acs_ref, *, nchunk):
    a_head = -jnp.exp(alog_ref[...]) * LOG2E
    ii = lax.broadcasted_iota(jnp.int32, (CHUNK, CHUNK), 0)
    jj = lax.broadcasted_iota(jnp.int32, (CHUNK, CHUNK), 1)
    upper = (ii <= jj).astype(F32)
    lower = (ii >= jj).astype(F32)
    is_fwd = (ii & 7) < SSD_R
    for c in range(nchunk):
        t = p_ref[c * CHUNK:(c + 1) * CHUNK, :]
        tt = t.T + bias_ref[...]
        dtv = jnp.maximum(tt, 0.0) + jnp.log1p(jnp.exp(-jnp.abs(tt)))
        a = dtv * a_head
        pre = jnp.dot(a, upper, precision=lax.Precision.HIGHEST, preferred_element_type=F32)
        suf = jnp.dot(a, lower, precision=lax.Precision.HIGHEST, preferred_element_type=F32)
        dt_ref[c] = dtv
        acs_ref[c] = jnp.where(is_fwd, pre, suf)


def _dtprep(proj, bias_col, alog_col):
    nchunk = 4
    tm = nchunk * CHUNK
    nct = N_TOK // CHUNK
    out = jax.ShapeDtypeStruct((nct, LANE, CHUNK), F32)
    return pl.pallas_call(
        functools.partial(_dtprep_kernel, nchunk=nchunk),
        grid=(N_TOK // tm,),
        in_specs=[pl.BlockSpec((tm, LANE), lambda i: (i, 0)),
                  pl.BlockSpec((LANE, 1), lambda i: (0, 0)),
                  pl.BlockSpec((LANE, 1), lambda i: (0, 0))],
        out_specs=[pl.BlockSpec((nchunk, LANE, CHUNK), lambda i: (i, 0, 0)),
                   pl.BlockSpec((nchunk, LANE, CHUNK), lambda i: (i, 0, 0))],
        out_shape=[out, out],
        compiler_params=_params(("arbitrary",)),
        name="dtprep",
    )(proj, bias_col, alog_col)


def _rope(x, cos, sa, sb):
    return x * cos + pltpu.roll(x, HD - ROPE_F, axis=1) * sa + pltpu.roll(x, ROPE_F, axis=1) * sb


def _softmax_pv(q4, k_b, v_b):
    s = lax.dot_general(q4.astype(BF16), k_b, (((1,), (1,)), ((), ())), preferred_element_type=F32)
    m = jnp.max(s, axis=-1, keepdims=True)
    p = jnp.exp(s - m)
    l = jnp.sum(p, axis=-1, keepdims=True)
    o = jnp.dot(p.astype(BF16), v_b, preferred_element_type=F32)
    return o / l


def _attn_ctx_kernel(q_ref, k_ref, v_ref, qn_ref, kn_ref, o_ref, ko_ref, vo_ref):
    k = _rms(k_ref[...].astype(F32), kn_ref[...])
    v = v_ref[...]
    ko_ref[...] = k
    vo_ref[...] = v.astype(F32)
    q = q_ref[...].astype(F32)
    qg = qn_ref[...] * (HD ** -0.5)
    q4 = jnp.concatenate([_rms(q[:, h * HD:(h + 1) * HD], qg) for h in range(REP)], axis=0)
    o = _softmax_pv(q4, k.astype(BF16), v)
    for h in range(REP):
        o_ref[:, h * HD:(h + 1) * HD] = o[h * SEQ:(h + 1) * SEQ].astype(BF16)


def _attn_ctx(proj, q_norm, k_norm):
    qw = REP * HD
    kv_out = jax.ShapeDtypeStruct((BATCH, SEQ, KV_DIM), F32)
    return pl.pallas_call(
        _attn_ctx_kernel,
        grid=(BATCH, H_KV),
        in_specs=[pl.BlockSpec((SEQ, qw), lambda b, g: (b, COL_Q // qw + g)),
                  pl.BlockSpec((SEQ, HD), lambda b, g: (b, COL_K // HD + g)),
                  pl.BlockSpec((SEQ, HD), lambda b, g: (b, COL_V // HD + g)),
                  pl.BlockSpec((1, HD), lambda b, g: (0, 0)),
                  pl.BlockSpec((1, HD), lambda b, g: (0, 0))],
        out_specs=[pl.BlockSpec((SEQ, qw), lambda b, g: (b, g)),
                   pl.BlockSpec((None, SEQ, HD), lambda b, g: (b, 0, g)),
                   pl.BlockSpec((None, SEQ, HD), lambda b, g: (b, 0, g))],
        out_shape=[jax.ShapeDtypeStruct((N_TOK, Q_DIM), BF16), kv_out, kv_out],
        compiler_params=_params(("arbitrary", "arbitrary")),
        name="attn_ctx",
    )(proj, proj, proj, q_norm, k_norm)


def _attn_lat_kernel(q_ref, k_ref, v_ref, ck_ref, cv_ref, cos_ref, sa_ref, sb_ref, qn_ref, kn_ref, prev_ref,
                     o_ref, kall, vall, *, tq):
    del prev_ref
    qi = pl.program_id(2)

    @pl.when(qi == 0)
    def _():
        k = _rope(_rms(k_ref[...].astype(F32), kn_ref[...]), cos_ref[...], sa_ref[...], sb_ref[...])
        kall[0:PAST_LEN, :] = ck_ref[...].astype(BF16)
        kall[PAST_LEN:, :] = k.astype(BF16)
        vall[0:PAST_LEN, :] = cv_ref[...].astype(BF16)
        vall[PAST_LEN:, :] = v_ref[...]

    r0 = pl.multiple_of(qi * tq, tq)
    cos = cos_ref[pl.ds(r0, tq), :]
    sa = sa_ref[pl.ds(r0, tq), :]
    sb = sb_ref[pl.ds(r0, tq), :]
    qg = qn_ref[...] * (HD ** -0.5)
    for h0 in range(0, REP, ATTN_HEADS_PER_CHAIN):
        hs = range(h0, h0 + ATTN_HEADS_PER_CHAIN)
        qh = jnp.concatenate(
            [_rope(_rms(q_ref[:, h * HD:(h + 1) * HD].astype(F32), qg), cos, sa, sb) for h in hs], axis=0)
        o = _softmax_pv(qh, kall[...], vall[...])
        for n, h in enumerate(hs):
            o_ref[:, h * HD:(h + 1) * HD] = o[n * tq:(n + 1) * tq].astype(BF16)


def _attn_lat(proj, cache_k, cache_v, cos, sa, sb, q_norm, k_norm, attn_prev):
    tq = 256
    qw = REP * HD
    nq = DEC_SEQ // tq
    row0 = N_CTX // tq
    tab = pl.BlockSpec((DEC_SEQ, HD), lambda b, g, i: (0, 0))
    return pl.pallas_call(
        functools.partial(_attn_lat_kernel, tq=tq),
        grid=(DEC_BATCH, H_KV, nq),
        in_specs=[pl.BlockSpec((tq, qw), lambda b, g, i: (row0 + b * nq + i, COL_Q // qw + g)),
                  pl.BlockSpec((DEC_SEQ, HD), lambda b, g, i: (N_CTX // DEC_SEQ + b, COL_K // HD + g)),
                  pl.BlockSpec((DEC_SEQ, HD), lambda b, g, i: (N_CTX // DEC_SEQ + b, COL_V // HD + g)),
                  pl.BlockSpec((None, PAST_LEN, HD), lambda b, g, i: (b, 0, g)),
                  pl.BlockSpec((None, PAST_LEN, HD), lambda b, g, i: (b, 0, g)),
                  tab, tab, tab,
                  pl.BlockSpec((1, HD), lambda b, g, i: (0, 0)),
                  pl.BlockSpec((1, HD), lambda b, g, i: (0, 0)),
                  pl.BlockSpec(memory_space=pl.ANY)],
        out_specs=pl.BlockSpec((tq, qw), lambda b, g, i: (row0 + b * nq + i, g)),
        out_shape=jax.ShapeDtypeStruct((N_TOK, Q_DIM), BF16),
        input_output_aliases={10: 0},
        scratch_shapes=[pltpu.VMEM((PAST_LEN + DEC_SEQ, HD), BF16),
                        pltpu.VMEM((PAST_LEN + DEC_SEQ, HD), BF16)],
        compiler_params=_params(("arbitrary", "arbitrary", "arbitrary")),
        name="attn_lat",
    )(proj, proj, proj, cache_k, cache_v, cos, sa, sb, q_norm, k_norm, attn_prev)


def _ssd_kernel(*refs, seq, has_h0, emit_state, has_prev):
    (x_ref, b_ref, c_ref, dt_ref, acs_ref, cwx_ref, cwb_ref, cwc_ref,
     cbx_ref, cbb_ref, cbc_ref, dsk_ref) = refs[:12]
    pos = 12
    h0_ref = None
    if has_h0:
        h0_ref = refs[pos]
        pos += 1
    if has_prev:
        pos += 1
    y_ref = refs[pos]
    pos += 1
    hout_ref = None
    if emit_state:
        hout_ref = refs[pos]
        pos += 1
    pad, xst, btf, cbf, csb, acc, ht, yacc = refs[pos:]
    nc = seq // CHUNK
    half = SSD_R * SSD_P // 2
    halo = 8

    ii = lax.broadcasted_iota(jnp.int32, (CHUNK, CHUNK), 0)
    jj = lax.broadcasted_iota(jnp.int32, (CHUNK, CHUNK), 1)
    lane_lo = jj < SSD_P

    def pair(a, b):
        return jnp.where(lane_lo, a, b)

    pad[0:halo, :] = jnp.zeros((halo, 4 * LANE), F32)
    pad[halo + seq:, :] = jnp.zeros((halo, 4 * LANE), F32)
    pad[halo:halo + seq, 0:2 * LANE] = x_ref[...].astype(F32)
    pad[halo:halo + seq, 2 * LANE:3 * LANE] = b_ref[...].astype(F32)
    pad[halo:halo + seq, 3 * LANE:] = c_ref[...].astype(F32)

    segs = [(cwx_ref, cbx_ref, 0), (cwx_ref, cbx_ref, LANE), (cwb_ref, cbb_ref, 0), (cwc_ref, cbc_ref, 0)]
    for rb in range(nc):
        vals = []
        for blk, (w_ref, bias_ref, lo) in enumerate(segs):
            a = None
            for k in range(CONV_K):
                s = halo + rb * CHUNK + k - CONV_K // 2
                term = pad[s:s + CHUNK, blk * LANE:(blk + 1) * LANE] * w_ref[k:k + 1, lo:lo + LANE]
                a = term if a is None else a + term
            a = a + bias_ref[:, lo:lo + LANE]
            vals.append(a * jax.nn.sigmoid(a))
        for p in range(2):
            v = vals[p]
            yacc[rb * CHUNK:(rb + 1) * CHUNK, p * LANE:(p + 1) * LANE] = v * dsk_ref[:, p * LANE:(p + 1) * LANE]
            vh = v.astype(BF16)
            zh = jnp.zeros_like(vh)
            xst[rb, p, 0:CHUNK, :] = jnp.where(lane_lo, vh, zh)
            xst[rb, p, CHUNK:, :] = jnp.where(lane_lo, zh, vh)
        vb = vals[2]
        vc = vals[3].astype(BF16)
        btf[rb] = vb.T
        csb[rb] = vc
        cbf[rb] = lax.dot_general(vc, vb.astype(BF16), (((1,), (1,)), ((), ())), preferred_element_type=F32)
        acc[rb] = jnp.concatenate([acs_ref[rb]] * (CHUNK // 8), axis=0).T

    for d in range(2):
        if has_h0:
            ht[d, :, 0:half] = h0_ref[d, 0:half, :].T
            ht[d, :, half:] = h0_ref[d, half:, :].T
        else:
            ht[d] = jnp.zeros((SSD_N, 2 * half), F32)

    def chunk(c, d):
        r0 = pl.multiple_of(c * CHUNK, CHUNK)
        cbm = cbf[c]
        bt = btf[c]
        cc = csb[c]
        acol = acc[c]
        arow = acs_ref[c]
        drow = dt_ref[c]
        h_prev = ht[d]
        yoff = jnp.dot(cc, h_prev.astype(BF16), preferred_element_type=F32)
        mask = (ii >= jj) if d == 0 else (ii <= jj)
        cbm = jnp.where(mask, cbm, 0.0)
        last = CHUNK - 1 if d == 0 else 0
        ms, bws, eas, cds = [], [], [], []
        for r in range(SSD_R):
            col = d * SSD_R + r
            acb = jnp.broadcast_to(acol[:, col:col + 1], (CHUNK, CHUNK))
            ar = arow[col:col + 1, :]
            dr = drow[col:col + 1, :]
            dec = jnp.exp2(jnp.minimum(acb - ar, 0.0))
            ms.append((cbm * dec * dr).astype(BF16))
            eas.append(jnp.exp2(acb))
            a_last = ar[:, last:last + 1]
            bws.append((bt * (dr * jnp.exp2(a_last - ar))).astype(BF16))
            cds.append(jnp.exp2(a_last))
        ys, sts, cdm = [], [], []
        for p in range(2):
            r_a, r_b = 2 * p, 2 * p + 1
            x2 = xst[c, p]
            yd = jnp.dot(jnp.concatenate([ms[r_a], ms[r_b]], axis=1), x2, preferred_element_type=F32)
            ys.append(yd + yoff[:, p * LANE:(p + 1) * LANE] * pair(eas[r_a], eas[r_b]))
            sts.append(jnp.dot(jnp.concatenate([bws[r_a], bws[r_b]], axis=1), x2, preferred_element_type=F32))
            cdm.append(pair(cds[r_a], cds[r_b]))
        ht[d] = h_prev * jnp.concatenate(cdm, axis=1) + jnp.concatenate(sts, axis=1)
        yacc[pl.ds(r0, CHUNK), :] += jnp.concatenate(ys, axis=1)

    def body(t, carry):
        chunk(t, 0)
        chunk(nc - 1 - t, 1)
        return carry

    lax.fori_loop(0, nc, body, 0, unroll=True)
    y_ref[...] = yacc[...].astype(BF16)

    if emit_state:
        for d in range(2):
            hout_ref[d, 0:half, :] = ht[d, :, 0:half].T
            hout_ref[d, half:, :] = ht[d, :, half:].T


def _ssd(proj, dt_t, acs_t, conv_w, conv_b, dskip, h0, y_prev, *, seq, nbatch, row_blk0, emit_state):
    gw = SSD_R * SSD_P
    nc = seq // CHUNK
    has_h0, has_prev = h0 is not None, y_prev is not None
    xb0, bb0, cb0 = COL_X // gw, COL_B // SSD_N, COL_C // SSD_N
    in_specs = [
        pl.BlockSpec((seq, gw), lambda b, g: (row_blk0 + b, xb0 + g)),
        pl.BlockSpec((seq, SSD_N), lambda b, g: (row_blk0 + b, bb0 + g)),
        pl.BlockSpec((seq, SSD_N), lambda b, g: (row_blk0 + b, cb0 + g)),
        pl.BlockSpec((nc, 8, CHUNK), lambda b, g: (row_blk0 + b, g, 0)),
        pl.BlockSpec((nc, 8, CHUNK), lambda b, g: (row_blk0 + b, g, 0)),
        pl.BlockSpec((CONV_K, gw), lambda b, g: (0, g)),
        pl.BlockSpec((CONV_K, SSD_N), lambda b, g: (0, D_INNER // SSD_N + g)),
        pl.BlockSpec((CONV_K, SSD_N), lambda b, g: (0, (D_INNER + BC_DIM) // SSD_N + g)),
        pl.BlockSpec((1, gw), lambda b, g: (0, g)),
        pl.BlockSpec((1, SSD_N), lambda b, g: (0, D_INNER // SSD_N + g)),
        pl.BlockSpec((1, SSD_N), lambda b, g: (0, (D_INNER + BC_DIM) // SSD_N + g)),
        pl.BlockSpec((1, gw), lambda b, g: (0, g)),
    ]
    args = [proj, proj, proj, dt_t, acs_t, conv_w, conv_w, conv_w, conv_b, conv_b, conv_b, dskip]
    state_spec = pl.BlockSpec((None, 2, gw, SSD_N), lambda b, g: (b, 0, g, 0))
    aliases = {}
    if has_h0:
        in_specs.append(state_spec)
        args.append(h0)
    if has_prev:
        aliases = {len(args): 0}
        in_specs.append(pl.BlockSpec(memory_space=pl.ANY))
        args.append(y_prev)
    out_specs = [pl.BlockSpec((seq, gw), lambda b, g: (row_blk0 + b, g))]
    out_shape = [jax.ShapeDtypeStruct((N_TOK, D_INNER), BF16)]
    if emit_state:
        out_specs.append(state_spec)
        out_shape.append(jax.ShapeDtypeStruct((nbatch, 2, H_SSD * SSD_P, SSD_N), F32))
    return pl.pallas_call(
        functools.partial(_ssd_kernel, seq=seq, has_h0=has_h0, emit_state=emit_state, has_prev=has_prev),
        grid=(nbatch, SSD_G),
        in_specs=in_specs,
        out_specs=out_specs,
        out_shape=out_shape,
        input_output_aliases=aliases,
        scratch_shapes=[pltpu.VMEM((seq + 16, 4 * LANE), F32),
                        pltpu.VMEM((nc, 2, 2 * CHUNK, LANE), BF16),
                        pltpu.VMEM((nc, SSD_N, CHUNK), F32),
                        pltpu.VMEM((nc, CHUNK, CHUNK), F32),
                        pltpu.VMEM((nc, CHUNK, SSD_N), BF16),
                        pltpu.VMEM((nc, CHUNK, LANE), F32),
                        pltpu.VMEM((2, SSD_N, gw), F32),
                        pltpu.VMEM((seq, gw), F32)],
        compiler_params=_params(("arbitrary", "arbitrary")),
        name="ssd_lat" if has_h0 else "ssd_ctx",
    )(*args)


def _merge_kernel(attn_ref, y_ref, z_ref, g_ref, xp_ref, xl_ref, mod_ref, nrm_ref, wa_ref, ws_ref, wo_ref,
                  l1g_ref, l1b_ref, o_ref, *, tm):
    is_ctx = pl.program_id(0) < N_CTX // tm
    for r0 in range(0, tm, MERGE_CHAIN_ROWS):
        rows = slice(r0, r0 + MERGE_CHAIN_ROWS)
        z = z_ref[rows, :].astype(F32)
        yg = y_ref[rows, :].astype(F32) * (z * jax.nn.sigmoid(z))
        yn = yg * lax.rsqrt(jnp.mean(yg * yg, axis=-1, keepdims=True) + EPS) * nrm_ref[...]
        pa = jnp.dot(attn_ref[rows, :], wa_ref[...], preferred_element_type=F32)
        ps = jnp.dot(yn.astype(BF16), ws_ref[...], preferred_element_type=F32)
        g = g_ref[rows, :].astype(F32)
        merged = jax.nn.sigmoid(g[:, :D_MODEL]) * pa + jax.nn.sigmoid(g[:, D_MODEL:]) * ps
        mix = jnp.dot(merged.astype(BF16), wo_ref[...], preferred_element_type=F32)
        x = jnp.where(is_ctx, xp_ref[rows, :], xl_ref[rows, :])
        x1 = _ln(ALPHA * x + mod_ref[2:3, :] * mix)
        o_ref[rows, :] = x1 * l1g_ref[...] + l1b_ref[...]


def _merge(attn, yssd, proj, xp, xl, mod, ssd_norm, wa, ws, wo, l1g, l1b):
    tm = 256
    const = lambda shape: pl.BlockSpec(shape, lambda i: (0, 0))
    return pl.pallas_call(
        functools.partial(_merge_kernel, tm=tm),
        grid=(N_TOK // tm,),
        in_specs=[pl.BlockSpec((tm, Q_DIM), lambda i: (i, 0)),
                  pl.BlockSpec((tm, D_INNER), lambda i: (i, 0)),
                  pl.BlockSpec((tm, D_INNER), lambda i: (i, COL_Z // D_INNER)),
                  pl.BlockSpec((tm, 2 * D_MODEL), lambda i: (i, COL_GATE // (2 * D_MODEL))),
                  pl.BlockSpec((tm, D_MODEL), lambda i: (_ctx_blk(i, tm), 0)),
                  pl.BlockSpec((tm, D_MODEL), lambda i: (_lat_blk(i, tm), 0)),
                  pl.BlockSpec((None, 6, D_MODEL), lambda i: (_mod_row(i, tm), 0, 0)),
                  const((1, D_INNER)), const((Q_DIM, D_MODEL)), const((D_INNER, D_MODEL)),
                  const((D_MODEL, D_MODEL)), const((1, D_MODEL)), const((1, D_MODEL))],
        out_specs=pl.BlockSpec((tm, D_MODEL), lambda i: (i, 0)),
        out_shape=jax.ShapeDtypeStruct((N_TOK, D_MODEL), F32),
        compiler_params=_params(("arbitrary",)),
        name="merge",
    )(attn, yssd, proj, proj, xp, xl, mod, ssd_norm, wa, ws, wo, l1g, l1b)


def _mlp_kernel(x1_ref, mod_ref, w1_ref, b1_ref, w2_ref, b2_ref, l2g_ref, l2b_ref, op_ref, ol_ref,
                h_scr, acc_scr, *, tm):
    i = pl.program_id(0)
    j = pl.program_id(1)

    @pl.when(j == 0)
    def _():
        h = _ln(x1_ref[...]) * (1.0 + mod_ref[4:5, :]) + mod_ref[3:4, :]
        h_scr[...] = h.astype(BF16)
        acc_scr[...] = jnp.zeros_like(acc_scr)

    u = jnp.dot(h_scr[...], w1_ref[...], preferred_element_type=F32) + b1_ref[...]
    u = jnp.maximum(u, 0.0)
    acc_scr[...] += jnp.dot((u * u).astype(BF16), w2_ref[...], preferred_element_type=F32)

    def result():
        ff = acc_scr[...] + b2_ref[...]
        x2 = _ln(ALPHA * x1_ref[...] + mod_ref[5:6, :] * ff)
        return x2 * l2g_ref[...] + l2b_ref[...]

    last = j == pl.num_programs(1) - 1
    is_ctx = i < N_CTX // tm

    @pl.when(last & is_ctx)
    def _():
        op_ref[...] = result()

    @pl.when(last & jnp.logical_not(is_ctx))
    def _():
        ol_ref[...] = result()


def _mlp(x1, mod, w1, b1, w2, b2, l2g, l2b):
    tm, tf = 1024, 512
    return pl.pallas_call(
        functools.partial(_mlp_kernel, tm=tm),
        grid=(N_TOK // tm, D_FF // tf),
        in_specs=[pl.BlockSpec((tm, D_MODEL), lambda i, j: (i, 0)),
                  pl.BlockSpec((None, 6, D_MODEL), lambda i, j: (_mod_row(i, tm), 0, 0)),
                  pl.BlockSpec((D_MODEL, tf), lambda i, j: (0, j)),
                  pl.BlockSpec((1, tf), lambda i, j: (0, j)),
                  pl.BlockSpec((tf, D_MODEL), lambda i, j: (j, 0)),
                  pl.BlockSpec((1, D_MODEL), lambda i, j: (0, 0)),
                  pl.BlockSpec((1, D_MODEL), lambda i, j: (0, 0)),
                  pl.BlockSpec((1, D_MODEL), lambda i, j: (0, 0))],
        out_specs=[pl.BlockSpec((tm, D_MODEL), lambda i, j: (_ctx_blk(i, tm), 0)),
                   pl.BlockSpec((tm, D_MODEL), lambda i, j: (_lat_blk(i, tm), 0))],
        out_shape=[jax.ShapeDtypeStruct((N_CTX, D_MODEL), F32), jax.ShapeDtypeStruct((N_LAT, D_MODEL), F32)],
        scratch_shapes=[pltpu.VMEM((tm, D_MODEL), BF16), pltpu.VMEM((tm, D_MODEL), F32)],
        compiler_params=_params(("arbitrary", "arbitrary")),
        name="mlp",
    )(x1, mod, w1, b1, w2, b2, l2g, l2b)


def _dt_perm():
    idx = []
    for n in range(DT_DIM):
        g, d, r = n // 8, (n % 8) // SSD_R, n % SSD_R
        idx.append(d * H_SSD + g * SSD_R + r)
    return jnp.asarray(idx, dtype=jnp.int32)


def _rope_tables():
    t = np.arange(DEC_SEQ)
    posn = np.stack([t // GRID_W, t % GRID_W], axis=-1).astype(np.float32)
    inv = (ROPE_THETA ** (-np.arange(ROPE_F, dtype=np.float32) / ROPE_F)).astype(np.float32)
    ang = posn[:, :, None] * inv
    cos, sin = np.cos(ang), np.sin(ang)
    zero = np.zeros_like(sin[:, 0])
    cos_t = np.concatenate([cos[:, 0], cos[:, 0], cos[:, 1], cos[:, 1]], axis=-1)
    sa_t = np.concatenate([-sin[:, 0], zero, -sin[:, 1], zero], axis=-1)
    sb_t = np.concatenate([zero, sin[:, 0], zero, sin[:, 1]], axis=-1)
    return tuple(jnp.asarray(a, dtype=F32) for a in (cos_t, sa_t, sb_t))


def kernel(x_prompt, x_sample, cache_k, cache_v, state_ssd, c, c_ctx, w_mod, b_mod, w_in, q_norm, k_norm, conv_w, conv_b, a_log, dt_bias, d_skip, ssd_norm, w_attn_o, w_ssd_o, w_out, ln1_g, ln1_b, w_mlp1, b_mlp1, w_mlp2, b_mlp2, ln2_g, ln2_b):
    l = 0
    perm = _dt_perm()
    w = w_in[l]
    o_q, o_k, o_v, o_z = 0, Q_DIM, Q_DIM + KV_DIM, Q_DIM + 2 * KV_DIM
    o_xbc = o_z + D_INNER
    o_dt = o_xbc + XBC_DIM
    o_gate = o_dt + DT_DIM
    w_dt = jnp.concatenate([jnp.take(w[:, o_dt:o_dt + DT_DIM], perm, axis=1),
                            jnp.zeros((D_MODEL, LANE - DT_DIM), F32)], axis=1).astype(BF16)
    w_main = jnp.concatenate(
        [w[:, o_z:o_z + D_INNER], w[:, o_gate:o_gate + 2 * D_MODEL], w[:, o_xbc:o_xbc + XBC_DIM],
         w[:, o_q:o_q + Q_DIM], w[:, o_k:o_k + KV_DIM], w[:, o_v:o_v + KV_DIM],
         jnp.zeros((D_MODEL, PROJ_W - COL_DT), F32)], axis=1).astype(BF16)
    pad = jnp.zeros((LANE - DT_DIM,), F32)
    bias_col = jnp.concatenate([jnp.take(dt_bias[l].reshape(-1), perm), pad]).reshape(LANE, 1)
    alog_col = jnp.concatenate([jnp.take(a_log[l].reshape(-1), perm), pad]).reshape(LANE, 1)
    dskip_row = jnp.repeat(d_skip[l], SSD_P).reshape(1, D_INNER)
    row = lambda v: v.reshape(1, -1)

    c_all = jnp.concatenate([c_ctx[None, :], c, jnp.zeros((8 - 1 - DEC_BATCH, D_MODEL), F32)], axis=0)
    mod = _modulation(c_all, w_mod[l], row(b_mod[l])).reshape(8, 6, D_MODEL)

    xp = x_prompt.reshape(N_CTX, D_MODEL)
    xl = x_sample.reshape(N_LAT, D_MODEL)
    proj, dt_raw = _inproj(xp, xl, mod, w_main, w_dt)
    dt_t, acs_t = _dtprep(dt_raw, bias_col, alog_col)

    qn, kn = row(q_norm[l]), row(k_norm[l])
    attn, k_new, v_new = _attn_ctx(proj, qn, kn)
    cos_t, sa_t, sb_t = _rope_tables()
    attn = _attn_lat(proj, cache_k[:, l].reshape(DEC_BATCH, PAST_LEN, KV_DIM),
                     cache_v[:, l].reshape(DEC_BATCH, PAST_LEN, KV_DIM), cos_t, sa_t, sb_t, qn, kn, attn)

    cw, cb = conv_w[l], row(conv_b[l])
    yssd, h_new = _ssd(proj, dt_t, acs_t, cw, cb, dskip_row, None, None, seq=SEQ, nbatch=BATCH,
                       row_blk0=0, emit_state=True)
    h0 = state_ssd[:, l].reshape(DEC_BATCH, 2, H_SSD * SSD_P, SSD_N)
    (yssd,) = _ssd(proj, dt_t, acs_t, cw, cb, dskip_row, h0, yssd, seq=DEC_SEQ, nbatch=DEC_BATCH,
                   row_blk0=N_CTX // DEC_SEQ, emit_state=False)

    x1 = _merge(attn, yssd, proj, xp, xl, mod, row(ssd_norm[l]), w_attn_o[l].astype(BF16),
                w_ssd_o[l].astype(BF16), w_out[l].astype(BF16), row(ln1_g[l]), row(ln1_b[l]))
    out_p, out_l = _mlp(x1, mod, w_mlp1[l].astype(BF16), row(b_mlp1[l]), w_mlp2[l].astype(BF16),
                        row(b_mlp2[l]), row(ln2_g[l]), row(ln2_b[l]))

    y_p = out_p.reshape(BATCH, SEQ, D_MODEL)
    y_s = out_l.reshape(DEC_BATCH, DEC_SEQ, D_MODEL)
    new_cache_k = k_new.reshape(BATCH, DEPTH, SEQ, H_KV, HD)
    new_cache_v = v_new.reshape(BATCH, DEPTH, SEQ, H_KV, HD)
    new_state = h_new.reshape(BATCH, DEPTH, 2, H_SSD, SSD_P, SSD_N)
    return (y_p, y_s, new_cache_k, new_cache_v, new_state)
```

```python
import functools

import jax
import jax.numpy as jnp
import numpy as np
from jax import lax
from jax.experimental import pallas as pl
from jax.experimental.pallas import tpu as pltpu

D_MODEL = 1024
BATCH = 16
SEQ = 256
DEC_BATCH = 4
DEC_SEQ = 1024
PAST_LEN = 512
GRID_W = 64
H_Q = 8
H_KV = 2
HD = 128
ROPE_F = HD // 4
ROPE_THETA = 10000.0
D_INNER = 2 * D_MODEL
SSD_P = 64
H_SSD = D_INNER // SSD_P
SSD_G = 8
SSD_R = H_SSD // SSD_G
SSD_N = 128
CONV_K = 5
CHUNK = 128
D_FF = 4 * D_MODEL
DEPTH = 1
ALPHA = (2.0 * DEPTH) ** 0.25
EPS = 1e-6
LOG2E = 1.4426950408889634
Q_DIM = H_Q * HD
KV_DIM = H_KV * HD
BC_DIM = SSD_G * SSD_N
XBC_DIM = D_INNER + 2 * BC_DIM
DT_DIM = 2 * H_SSD
REP = H_Q // H_KV

N_CTX = BATCH * SEQ
N_LAT = DEC_BATCH * DEC_SEQ
N_TOK = N_CTX + N_LAT

LANE = 128

COL_Z = 0
COL_GATE = COL_Z + D_INNER
COL_X = COL_GATE + 2 * D_MODEL
COL_B = COL_X + D_INNER
COL_C = COL_B + BC_DIM
COL_Q = COL_C + BC_DIM
COL_K = COL_Q + Q_DIM
COL_V = COL_K + KV_DIM
COL_DT = COL_V + KV_DIM
INPROJ_TN = 1280
PROJ_W = -(-COL_DT // INPROJ_TN) * INPROJ_TN

VMEM_LIMIT = 52 * 1024 * 1024
ATTN_HEADS_PER_CHAIN = 1

F32 = jnp.float32
BF16 = jnp.bfloat16


def _ln(x):
    mu = jnp.mean(x, axis=-1, keepdims=True)
    xc = x - mu
    var = jnp.mean(xc * xc, axis=-1, keepdims=True)
    return xc * lax.rsqrt(var + EPS)


def _rms(x, g):
    return x * lax.rsqrt(jnp.mean(x * x, axis=-1, keepdims=True) + EPS) * g


def _mod_row(i, tm):
    nct = N_CTX // tm
    tpl = DEC_SEQ // tm
    return jnp.where(i < nct, 0, 1 + jnp.maximum(i - nct, 0) // tpl)


def _ctx_blk(i, tm):
    return jnp.minimum(i, N_CTX // tm - 1)


def _lat_blk(i, tm):
    return jnp.maximum(i - N_CTX // tm, 0)


def _params(sem):
    return pltpu.CompilerParams(dimension_semantics=sem, vmem_limit_bytes=VMEM_LIMIT)


def _mod_kernel(c_ref, w_ref, b_ref, o_ref):
    c = c_ref[...]
    s = c * jax.nn.sigmoid(c)
    o_ref[...] = jnp.dot(s.astype(BF16), w_ref[...].astype(BF16), preferred_element_type=F32) + b_ref[...]


def _modulation(c_all, w_mod, b_mod):
    tn = 1536
    return pl.pallas_call(
        _mod_kernel,
        grid=(6 * D_MODEL // tn,),
        in_specs=[pl.BlockSpec((8, D_MODEL), lambda j: (0, 0)),
                  pl.BlockSpec((D_MODEL, tn), lambda j: (0, j)),
                  pl.BlockSpec((1, tn), lambda j: (0, j))],
        out_specs=pl.BlockSpec((8, tn), lambda j: (0, j)),
        out_shape=jax.ShapeDtypeStruct((8, 6 * D_MODEL), F32),
        compiler_params=_params(("arbitrary",)),
        name="modulation",
    )(c_all, w_mod, b_mod)


def _inproj_kernel(xp_ref, xl_ref, mod_ref, w_ref, wdt_ref, o_ref, dt_ref, h_scr, *, tm):
    @pl.when(pl.program_id(1) == 0)
    def _():
        x = jnp.where(pl.program_id(0) < N_CTX // tm, xp_ref[...], xl_ref[...])
        h = _ln(x) * (1.0 + mod_ref[1:2, :]) + mod_ref[0:1, :]
        h_scr[...] = h.astype(BF16)
        dt_ref[...] = jnp.dot(h_scr[...], wdt_ref[...], preferred_element_type=F32)

    o_ref[...] = jnp.dot(h_scr[...], w_ref[...], preferred_element_type=F32).astype(BF16)


def _inproj(xp, xl, mod, w_main, w_dt):
    tm, tn = 1024, INPROJ_TN
    return pl.pallas_call(
        functools.partial(_inproj_kernel, tm=tm),
        grid=(N_TOK // tm, PROJ_W // tn),
        in_specs=[pl.BlockSpec((tm, D_MODEL), lambda i, j: (_ctx_blk(i, tm), 0)),
                  pl.BlockSpec((tm, D_MODEL), lambda i, j: (_lat_blk(i, tm), 0)),
                  pl.BlockSpec((None, 6, D_MODEL), lambda i, j: (_mod_row(i, tm), 0, 0)),
                  pl.BlockSpec((D_MODEL, tn), lambda i, j: (0, j)),
                  pl.BlockSpec((D_MODEL, LANE), lambda i, j: (0, 0))],
        out_specs=[pl.BlockSpec((tm, tn), lambda i, j: (i, j)),
                   pl.BlockSpec((tm, LANE), lambda i, j: (i, 0))],
        out_shape=[jax.ShapeDtypeStruct((N_TOK, PROJ_W), BF16),
                   jax.ShapeDtypeStruct((N_TOK, LANE), F32)],
        scratch_shapes=[pltpu.VMEM((tm, D_MODEL), BF16)],
        compiler_params=_params(("arbitrary", "arbitrary")),
        name="inproj",
    )(xp, xl, mod, w_main, w_dt)


def _dtprep_kernel(p_ref, bias_ref, alog_ref, dt_ref, acs_ref, *, nchunk):
    a_head = -jnp.exp(alog_ref[...]) * LOG2E
    ii = lax.broadcasted_iota(jnp.int32, (CHUNK, CHUNK), 0)
    jj = lax.broadcasted_iota(jnp.int32, (CHUNK, CHUNK), 1)
    upper = (ii <= jj).astype(F32)
    lower = (ii >= jj).astype(F32)
    is_fwd = (ii & 7) < SSD_R
    for c in range(nchunk):
        t = p_ref[c * CHUNK:(c + 1) * CHUNK, :]
        tt = t.T + bias_ref[...]
        dtv = jnp.maximum(tt, 0.0) + jnp.log1p(jnp.exp(-jnp.abs(tt)))
        a = dtv * a_head
        pre = jnp.dot(a, upper, precision=lax.Precision.HIGHEST, preferred_element_type=F32)
        suf = jnp.dot(a, lower, precision=lax.Precision.HIGHEST, preferred_element_type=F32)
        dt_ref[c] = dtv
        acs_ref[c] = jnp.where(is_fwd, pre, suf)


def _dtprep(proj, bias_col, alog_col):
    nchunk = 4
    tm = nchunk * CHUNK
    nct = N_TOK // CHUNK
    out = jax.ShapeDtypeStruct((nct, LANE, CHUNK), F32)
    return pl.pallas_call(
        functools.partial(_dtprep_kernel, nchunk=nchunk),
        grid=(N_TOK // tm,),
        in_specs=[pl.BlockSpec((tm, LANE), lambda i: (i, 0)),
                  pl.BlockSpec((LANE, 1), lambda i: (0, 0)),
                  pl.BlockSpec((LANE, 1), lambda i: (0, 0))],
        out_specs=[pl.BlockSpec((nchunk, LANE, CHUNK), lambda i: (i, 0, 0)),
                   pl.BlockSpec((nchunk, LANE, CHUNK), lambda i: (i, 0, 0))],
        out_shape=[out, out],
        compiler_params=_params(("arbitrary",)),
        name="dtprep",
    )(proj, bias_col, alog_col)


def _rope(x, cos, sa, sb):
    return x * cos + pltpu.roll(x, HD - ROPE_F, axis=1) * sa + pltpu.roll(x, ROPE_F, axis=1) * sb


def _softmax_pv(q4, k_b, v_b):
    s = lax.dot_general(q4.astype(BF16), k_b, (((1,), (1,)), ((), ())), preferred_element_type=F32)
    m = jnp.max(s, axis=-1, keepdims=True)
    p = jnp.exp2(s - m)
    l = jnp.sum(p, axis=-1, keepdims=True)
    o = jnp.dot(p.astype(BF16), v_b, preferred_element_type=F32)
    return o / l


def _attn_ctx_kernel(q_ref, k_ref, v_ref, qn_ref, kn_ref, o_ref, ko_ref, vo_ref):
    k = _rms(k_ref[...].astype(F32), kn_ref[...])
    v = v_ref[...]
    ko_ref[...] = k
    vo_ref[...] = v.astype(F32)
    q = q_ref[...].astype(F32)
    qg = qn_ref[...] * (LOG2E * HD ** -0.5)
    q4 = jnp.concatenate([_rms(q[:, h * HD:(h + 1) * HD], qg) for h in range(REP)], axis=0)
    o = _softmax_pv(q4, k.astype(BF16), v)
    for h in range(REP):
        o_ref[:, h * HD:(h + 1) * HD] = o[h * SEQ:(h + 1) * SEQ].astype(BF16)


def _attn_ctx(proj, q_norm, k_norm):
    qw = REP * HD
    kv_out = jax.ShapeDtypeStruct((BATCH, SEQ, KV_DIM), F32)
    return pl.pallas_call(
        _attn_ctx_kernel,
        grid=(BATCH, H_KV),
        in_specs=[pl.BlockSpec((SEQ, qw), lambda b, g: (b, COL_Q // qw + g)),
                  pl.BlockSpec((SEQ, HD), lambda b, g: (b, COL_K // HD + g)),
                  pl.BlockSpec((SEQ, HD), lambda b, g: (b, COL_V // HD + g)),
                  pl.BlockSpec((1, HD), lambda b, g: (0, 0)),
                  pl.BlockSpec((1, HD), lambda b, g: (0, 0))],
        out_specs=[pl.BlockSpec((SEQ, qw), lambda b, g: (b, g)),
                   pl.BlockSpec((None, SEQ, HD), lambda b, g: (b, 0, g)),
                   pl.BlockSpec((None, SEQ, HD), lambda b, g: (b, 0, g))],
        out_shape=[jax.ShapeDtypeStruct((N_CTX, Q_DIM), BF16), kv_out, kv_out],
        compiler_params=_params(("arbitrary", "arbitrary")),
        name="attn_ctx",
    )(proj, proj, proj, q_norm, k_norm)


def _attn_lat_kernel(q_ref, k_ref, v_ref, ck_ref, cv_ref, cos_ref, sa_ref, sb_ref, qn_ref, kn_ref,
                     o_ref, kall, vall, *, tq):
    qi = pl.program_id(2)

    @pl.when(qi == 0)
    def _():
        k = _rope(_rms(k_ref[...].astype(F32), kn_ref[...]), cos_ref[...], sa_ref[...], sb_ref[...])
        kall[0:PAST_LEN, :] = ck_ref[...].astype(BF16)
        kall[PAST_LEN:, :] = k.astype(BF16)
        vall[0:PAST_LEN, :] = cv_ref[...].astype(BF16)
        vall[PAST_LEN:, :] = v_ref[...]

    r0 = pl.multiple_of(qi * tq, tq)
    cos = cos_ref[pl.ds(r0, tq), :]
    sa = sa_ref[pl.ds(r0, tq), :]
    sb = sb_ref[pl.ds(r0, tq), :]
    qg = qn_ref[...] * (LOG2E * HD ** -0.5)
    for h0 in range(0, REP, ATTN_HEADS_PER_CHAIN):
        hs = range(h0, h0 + ATTN_HEADS_PER_CHAIN)
        qh = jnp.concatenate(
            [_rope(_rms(q_ref[:, h * HD:(h + 1) * HD].astype(F32), qg), cos, sa, sb) for h in hs], axis=0)
        o = _softmax_pv(qh, kall[...], vall[...])
        for n, h in enumerate(hs):
            o_ref[:, h * HD:(h + 1) * HD] = o[n * tq:(n + 1) * tq].astype(BF16)


def _attn_lat(proj, cache_k, cache_v, cos, sa, sb, q_norm, k_norm):
    tq = 256
    qw = REP * HD
    nq = DEC_SEQ // tq
    row0 = N_CTX // tq
    tab = pl.BlockSpec((DEC_SEQ, HD), lambda b, g, i: (0, 0))
    return pl.pallas_call(
        functools.partial(_attn_lat_kernel, tq=tq),
        grid=(DEC_BATCH, H_KV, nq),
        in_specs=[pl.BlockSpec((tq, qw), lambda b, g, i: (row0 + b * nq + i, COL_Q // qw + g)),
                  pl.BlockSpec((DEC_SEQ, HD), lambda b, g, i: (N_CTX // DEC_SEQ + b, COL_K // HD + g)),
                  pl.BlockSpec((DEC_SEQ, HD), lambda b, g, i: (N_CTX // DEC_SEQ + b, COL_V // HD + g)),
                  pl.BlockSpec((None, PAST_LEN, HD), lambda b, g, i: (b, 0, g)),
                  pl.BlockSpec((None, PAST_LEN, HD), lambda b, g, i: (b, 0, g)),
                  tab, tab, tab,
                  pl.BlockSpec((1, HD), lambda b, g, i: (0, 0)),
                  pl.BlockSpec((1, HD), lambda b, g, i: (0, 0))],
        out_specs=pl.BlockSpec((tq, qw), lambda b, g, i: (b * nq + i, g)),
        out_shape=jax.ShapeDtypeStruct((N_LAT, Q_DIM), BF16),
        scratch_shapes=[pltpu.VMEM((PAST_LEN + DEC_SEQ, HD), BF16),
                        pltpu.VMEM((PAST_LEN + DEC_SEQ, HD), BF16)],
        compiler_params=_params(("arbitrary", "arbitrary", "arbitrary")),
        name="attn_lat",
    )(proj, proj, proj, cache_k, cache_v, cos, sa, sb, q_norm, k_norm)


def _ssd_kernel(*refs, seq, has_h0, emit_state):
    (x_ref, b_ref, c_ref, dt_ref, acs_ref, cwx_ref, cwb_ref, cwc_ref,
     cbx_ref, cbb_ref, cbc_ref, dsk_ref) = refs[:12]
    pos = 12
    h0_ref = None
    if has_h0:
        h0_ref = refs[pos]
        pos += 1
    y_ref = refs[pos]
    pos += 1
    hout_ref = None
    if emit_state:
        hout_ref = refs[pos]
        pos += 1
    pad, xst, btf, cbf, csb, acc, ht, yacc = refs[pos:]
    nc = seq // CHUNK
    half = SSD_R * SSD_P // 2
    halo = 8

    ii = lax.broadcasted_iota(jnp.int32, (CHUNK, CHUNK), 0)
    jj = lax.broadcasted_iota(jnp.int32, (CHUNK, CHUNK), 1)
    lane_lo = jj < SSD_P

    def pair(a, b):
        return jnp.where(lane_lo, a, b)

    pad[0:halo, :] = jnp.zeros((halo, 4 * LANE), F32)
    pad[halo + seq:, :] = jnp.zeros((halo, 4 * LANE), F32)
    pad[halo:halo + seq, 0:2 * LANE] = x_ref[...].astype(F32)
    pad[halo:halo + seq, 2 * LANE:3 * LANE] = b_ref[...].astype(F32)
    pad[halo:halo + seq, 3 * LANE:] = c_ref[...].astype(F32)

    segs = [(cwx_ref, cbx_ref, 0), (cwx_ref, cbx_ref, LANE), (cwb_ref, cbb_ref, 0), (cwc_ref, cbc_ref, 0)]
    for rb in range(nc):
        vals = []
        for blk, (w_ref, bias_ref, lo) in enumerate(segs):
            a = None
            for k in range(CONV_K):
                s = halo + rb * CHUNK + k - CONV_K // 2
                term = pad[s:s + CHUNK, blk * LANE:(blk + 1) * LANE] * w_ref[k:k + 1, lo:lo + LANE]
                a = term if a is None else a + term
            a = a + bias_ref[:, lo:lo + LANE]
            vals.append(a * jax.nn.sigmoid(a))
        for p in range(2):
            v = vals[p]
            yacc[rb * CHUNK:(rb + 1) * CHUNK, p * LANE:(p + 1) * LANE] = v * dsk_ref[:, p * LANE:(p + 1) * LANE]
            vh = v.astype(BF16)
            zh = jnp.zeros_like(vh)
            xst[rb, p, 0:CHUNK, :] = jnp.where(lane_lo, vh, zh)
            xst[rb, p, CHUNK:, :] = jnp.where(lane_lo, zh, vh)
        vb = vals[2]
        vc = vals[3].astype(BF16)
        btf[rb] = vb.T
        csb[rb] = vc
        cbf[rb] = lax.dot_general(vc, vb.astype(BF16), (((1,), (1,)), ((), ())), preferred_element_type=F32)
        acc[rb] = jnp.concatenate([acs_ref[rb]] * (CHUNK // 8), axis=0).T

    for d in range(2):
        if has_h0:
            ht[d, :, 0:half] = h0_ref[d, 0:half, :].T
            ht[d, :, half:] = h0_ref[d, half:, :].T
        else:
            ht[d] = jnp.zeros((SSD_N, 2 * half), F32)

    def chunk(c, d):
        r0 = pl.multiple_of(c * CHUNK, CHUNK)
        cbm = cbf[c]
        bt = btf[c]
        cc = csb[c]
        acol = acc[c]
        arow = acs_ref[c]
        drow = dt_ref[c]
        h_prev = ht[d]
        yoff = jnp.dot(cc, h_prev.astype(BF16), preferred_element_type=F32)
        mask = (ii >= jj) if d == 0 else (ii <= jj)
        cbm = jnp.where(mask, cbm, 0.0)
        last = CHUNK - 1 if d == 0 else 0
        ms, bws, eas, cds = [], [], [], []
        for r in range(SSD_R):
            col = d * SSD_R + r
            acb = jnp.broadcast_to(acol[:, col:col + 1], (CHUNK, CHUNK))
            ar = arow[col:col + 1, :]
            dr = drow[col:col + 1, :]
            dec = jnp.exp2(jnp.minimum(acb - ar, 0.0))
            ms.append((cbm * dec * dr).astype(BF16))
            eas.append(jnp.exp2(acb))
            a_last = ar[:, last:last + 1]
            bws.append((bt * (dr * jnp.exp2(a_last - ar))).astype(BF16))
            cds.append(jnp.exp2(a_last))
        ys, sts, cdm = [], [], []
        for p in range(2):
            r_a, r_b = 2 * p, 2 * p + 1
            x2 = xst[c, p]
            yd = jnp.dot(jnp.concatenate([ms[r_a], ms[r_b]], axis=1), x2, preferred_element_type=F32)
            ys.append(yd + yoff[:, p * LANE:(p + 1) * LANE] * pair(eas[r_a], eas[r_b]))
            sts.append(jnp.dot(jnp.concatenate([bws[r_a], bws[r_b]], axis=1), x2, preferred_element_type=F32))
            cdm.append(pair(cds[r_a], cds[r_b]))
        ht[d] = h_prev * jnp.concatenate(cdm, axis=1) + jnp.concatenate(sts, axis=1)
        yacc[pl.ds(r0, CHUNK), :] += jnp.concatenate(ys, axis=1)

    def body(t, carry):
        chunk(t, 0)
        chunk(nc - 1 - t, 1)
        return carry

    lax.fori_loop(0, nc, body, 0, unroll=True)
    y_ref[...] = yacc[...].astype(BF16)

    if emit_state:
        for d in range(2):
            hout_ref[d, 0:half, :] = ht[d, :, 0:half].T
            hout_ref[d, half:, :] = ht[d, :, half:].T


def _ssd(proj, dt_t, acs_t, conv_w, conv_b, dskip, h0, *, seq, nbatch, row_blk0, emit_state):
    gw = SSD_R * SSD_P
    nc = seq // CHUNK
    has_h0 = h0 is not None
    xb0, bb0, cb0 = COL_X // gw, COL_B // SSD_N, COL_C // SSD_N
    in_specs = [
        pl.BlockSpec((seq, gw), lambda b, g: (row_blk0 + b, xb0 + g)),
        pl.BlockSpec((seq, SSD_N), lambda b, g: (row_blk0 + b, bb0 + g)),
        pl.BlockSpec((seq, SSD_N), lambda b, g: (row_blk0 + b, cb0 + g)),
        pl.BlockSpec((nc, 8, CHUNK), lambda b, g: (row_blk0 + b, g, 0)),
        pl.BlockSpec((nc, 8, CHUNK), lambda b, g: (row_blk0 + b, g, 0)),
        pl.BlockSpec((CONV_K, gw), lambda b, g: (0, g)),
        pl.BlockSpec((CONV_K, SSD_N), lambda b, g: (0, D_INNER // SSD_N + g)),
        pl.BlockSpec((CONV_K, SSD_N), lambda b, g: (0, (D_INNER + BC_DIM) // SSD_N + g)),
        pl.BlockSpec((1, gw), lambda b, g: (0, g)),
        pl.BlockSpec((1, SSD_N), lambda b, g: (0, D_INNER // SSD_N + g)),
        pl.BlockSpec((1, SSD_N), lambda b, g: (0, (D_INNER + BC_DIM) // SSD_N + g)),
        pl.BlockSpec((1, gw), lambda b, g: (0, g)),
    ]
    args = [proj, proj, proj, dt_t, acs_t, conv_w, conv_w, conv_w, conv_b, conv_b, conv_b, dskip]
    state_spec = pl.BlockSpec((None, 2, gw, SSD_N), lambda b, g: (b, 0, g, 0))
    if has_h0:
        in_specs.append(state_spec)
        args.append(h0)
    out_specs = [pl.BlockSpec((seq, gw), lambda b, g: (b, g))]
    out_shape = [jax.ShapeDtypeStruct((nbatch * seq, D_INNER), BF16)]
    if emit_state:
        out_specs.append(state_spec)
        out_shape.append(jax.ShapeDtypeStruct((nbatch, 2, H_SSD * SSD_P, SSD_N), F32))
    return pl.pallas_call(
        functools.partial(_ssd_kernel, seq=seq, has_h0=has_h0, emit_state=emit_state),
        grid=(nbatch, SSD_G),
        in_specs=in_specs,
        out_specs=out_specs,
        out_shape=out_shape,
        scratch_shapes=[pltpu.VMEM((seq + 16, 4 * LANE), F32),
                        pltpu.VMEM((nc, 2, 2 * CHUNK, LANE), BF16),
                        pltpu.VMEM((nc, SSD_N, CHUNK), F32),
                        pltpu.VMEM((nc, CHUNK, CHUNK), F32),
                        pltpu.VMEM((nc, CHUNK, SSD_N), BF16),
                        pltpu.VMEM((nc, CHUNK, LANE), F32),
                        pltpu.VMEM((2, SSD_N, gw), F32),
                        pltpu.VMEM((seq, gw), F32)],
        compiler_params=_params(("arbitrary", "arbitrary")),
        name="ssd_lat" if has_h0 else "ssd_ctx",
    )(*args)


def _merge_kernel(ac_ref, al_ref, yc_ref, yl_ref, z_ref, g_ref, xp_ref, xl_ref, mod_ref, nrm_ref,
                  wa_ref, ws_ref, wo_ref, l1g_ref, l1b_ref, o_ref, *, tm):
    is_ctx = pl.program_id(0) < N_CTX // tm
    attn = jnp.where(is_ctx, ac_ref[...], al_ref[...])
    y = jnp.where(is_ctx, yc_ref[...], yl_ref[...])
    x = jnp.where(is_ctx, xp_ref[...], xl_ref[...])
    z = z_ref[...].astype(F32)
    yg = y.astype(F32) * (z * jax.nn.sigmoid(z))
    yn = yg * lax.rsqrt(jnp.mean(yg * yg, axis=-1, keepdims=True) + EPS) * nrm_ref[...]
    pa = jnp.dot(attn, wa_ref[...], preferred_element_type=F32)
    ps = jnp.dot(yn.astype(BF16), ws_ref[...], preferred_element_type=F32)
    g = g_ref[...].astype(F32)
    merged = jax.nn.sigmoid(g[:, :D_MODEL]) * pa + jax.nn.sigmoid(g[:, D_MODEL:]) * ps
    mix = jnp.dot(merged.astype(BF16), wo_ref[...], preferred_element_type=F32)
    x1 = _ln(ALPHA * x + mod_ref[2:3, :] * mix)
    o_ref[...] = x1 * l1g_ref[...] + l1b_ref[...]


def _merge(attn_c, attn_l, y_c, y_l, proj, xp, xl, mod, ssd_norm, wa, ws, wo, l1g, l1b):
    tm = 256
    const = lambda shape: pl.BlockSpec(shape, lambda i: (0, 0))
    ctx_rows = lambda w: pl.BlockSpec((tm, w), lambda i: (_ctx_blk(i, tm), 0))
    lat_rows = lambda w: pl.BlockSpec((tm, w), lambda i: (_lat_blk(i, tm), 0))
    return pl.pallas_call(
        functools.partial(_merge_kernel, tm=tm),
        grid=(N_TOK // tm,),
        in_specs=[ctx_rows(Q_DIM), lat_rows(Q_DIM), ctx_rows(D_INNER), lat_rows(D_INNER),
                  pl.BlockSpec((tm, D_INNER), lambda i: (i, COL_Z // D_INNER)),
                  pl.BlockSpec((tm, 2 * D_MODEL), lambda i: (i, COL_GATE // (2 * D_MODEL))),
                  pl.BlockSpec((tm, D_MODEL), lambda i: (_ctx_blk(i, tm), 0)),
                  pl.BlockSpec((tm, D_MODEL), lambda i: (_lat_blk(i, tm), 0)),
                  pl.BlockSpec((None, 6, D_MODEL), lambda i: (_mod_row(i, tm), 0, 0)),
                  const((1, D_INNER)), const((Q_DIM, D_MODEL)), const((D_INNER, D_MODEL)),
                  const((D_MODEL, D_MODEL)), const((1, D_MODEL)), const((1, D_MODEL))],
        out_specs=pl.BlockSpec((tm, D_MODEL), lambda i: (i, 0)),
        out_shape=jax.ShapeDtypeStruct((N_TOK, D_MODEL), F32),
        compiler_params=_params(("arbitrary",)),
        name="merge",
    )(attn_c, attn_l, y_c, y_l, proj, proj, xp, xl, mod, ssd_norm, wa, ws, wo, l1g, l1b)


def _mlp_kernel(x1_ref, mod_ref, w1_ref, b1_ref, w2_ref, b2_ref, l2g_ref, l2b_ref, op_ref, ol_ref,
                h_scr, acc_scr, *, tm):
    i = pl.program_id(0)
    j = pl.program_id(1)

    @pl.when(j == 0)
    def _():
        h = _ln(x1_ref[...]) * (1.0 + mod_ref[4:5, :]) + mod_ref[3:4, :]
        h_scr[...] = h.astype(BF16)
        acc_scr[...] = jnp.zeros_like(acc_scr)

    u = jnp.dot(h_scr[...], w1_ref[...].astype(BF16), preferred_element_type=F32) + b1_ref[...]
    u = jnp.maximum(u, 0.0)
    acc_scr[...] += jnp.dot((u * u).astype(BF16), w2_ref[...].astype(BF16), preferred_element_type=F32)

    def result():
        ff = acc_scr[...] + b2_ref[...]
        x2 = _ln(ALPHA * x1_ref[...] + mod_ref[5:6, :] * ff)
        return x2 * l2g_ref[...] + l2b_ref[...]

    last = j == pl.num_programs(1) - 1
    is_ctx = i < N_CTX // tm

    @pl.when(last & is_ctx)
    def _():
        op_ref[...] = result()

    @pl.when(last & jnp.logical_not(is_ctx))
    def _():
        ol_ref[...] = result()


def _mlp(x1, mod, w1, b1, w2, b2, l2g, l2b):
    tm, tf = 1024, 512
    return pl.pallas_call(
        functools.partial(_mlp_kernel, tm=tm),
        grid=(N_TOK // tm, D_FF // tf),
        in_specs=[pl.BlockSpec((tm, D_MODEL), lambda i, j: (i, 0)),
                  pl.BlockSpec((None, 6, D_MODEL), lambda i, j: (_mod_row(i, tm), 0, 0)),
                  pl.BlockSpec((D_MODEL, tf), lambda i, j: (0, j)),
                  pl.BlockSpec((1, tf), lambda i, j: (0, j)),
                  pl.BlockSpec((tf, D_MODEL), lambda i, j: (j, 0)),
                  pl.BlockSpec((1, D_MODEL), lambda i, j: (0, 0)),
                  pl.BlockSpec((1, D_MODEL), lambda i, j: (0, 0)),
                  pl.BlockSpec((1, D_MODEL), lambda i, j: (0, 0))],
        out_specs=[pl.BlockSpec((tm, D_MODEL), lambda i, j: (_ctx_blk(i, tm), 0)),
                   pl.BlockSpec((tm, D_MODEL), lambda i, j: (_lat_blk(i, tm), 0))],
        out_shape=[jax.ShapeDtypeStruct((N_CTX, D_MODEL), F32), jax.ShapeDtypeStruct((N_LAT, D_MODEL), F32)],
        scratch_shapes=[pltpu.VMEM((tm, D_MODEL), BF16), pltpu.VMEM((tm, D_MODEL), F32)],
        compiler_params=_params(("arbitrary", "arbitrary")),
        name="mlp",
    )(x1, mod, w1, b1, w2, b2, l2g, l2b)


def _dt_perm():
    idx = []
    for n in range(DT_DIM):
        g, d, r = n // 8, (n % 8) // SSD_R, n % SSD_R
        idx.append(d * H_SSD + g * SSD_R + r)
    return jnp.asarray(idx, dtype=jnp.int32)


def _rope_tables():
    t = np.arange(DEC_SEQ)
    posn = np.stack([t // GRID_W, t % GRID_W], axis=-1).astype(np.float32)
    inv = (ROPE_THETA ** (-np.arange(ROPE_F, dtype=np.float32) / ROPE_F)).astype(np.float32)
    ang = posn[:, :, None] * inv
    cos, sin = np.cos(ang), np.sin(ang)
    zero = np.zeros_like(sin[:, 0])
    cos_t = np.concatenate([cos[:, 0], cos[:, 0], cos[:, 1], cos[:, 1]], axis=-1)
    sa_t = np.concatenate([-sin[:, 0], zero, -sin[:, 1], zero], axis=-1)
    sb_t = np.concatenate([zero, sin[:, 0], zero, sin[:, 1]], axis=-1)
    return tuple(jnp.asarray(a, dtype=F32) for a in (cos_t, sa_t, sb_t))


def kernel(x_prompt, x_sample, cache_k, cache_v, state_ssd, c, c_ctx, w_mod, b_mod, w_in, q_norm, k_norm, conv_w, conv_b, a_log, dt_bias, d_skip, ssd_norm, w_attn_o, w_ssd_o, w_out, ln1_g, ln1_b, w_mlp1, b_mlp1, w_mlp2, b_mlp2, ln2_g, ln2_b):
    l = 0
    perm = _dt_perm()
    w = w_in[l]
    o_q, o_k, o_v, o_z = 0, Q_DIM, Q_DIM + KV_DIM, Q_DIM + 2 * KV_DIM
    o_xbc = o_z + D_INNER
    o_dt = o_xbc + XBC_DIM
    o_gate = o_dt + DT_DIM
    w_dt = jnp.concatenate([jnp.take(w[:, o_dt:o_dt + DT_DIM], perm, axis=1),
                            jnp.zeros((D_MODEL, LANE - DT_DIM), F32)], axis=1).astype(BF16)
    w_main = jnp.concatenate(
        [w[:, o_z:o_z + D_INNER], w[:, o_gate:o_gate + 2 * D_MODEL], w[:, o_xbc:o_xbc + XBC_DIM],
         w[:, o_q:o_q + Q_DIM], w[:, o_k:o_k + KV_DIM], w[:, o_v:o_v + KV_DIM],
         jnp.zeros((D_MODEL, PROJ_W - COL_DT), F32)], axis=1).astype(BF16)
    pad = jnp.zeros((LANE - DT_DIM,), F32)
    bias_col = jnp.concatenate([jnp.take(dt_bias[l].reshape(-1), perm), pad]).reshape(LANE, 1)
    alog_col = jnp.concatenate([jnp.take(a_log[l].reshape(-1), perm), pad]).reshape(LANE, 1)
    dskip_row = jnp.repeat(d_skip[l], SSD_P).reshape(1, D_INNER)
    row = lambda v: v.reshape(1, -1)

    c_all = jnp.concatenate([c_ctx[None, :], c, jnp.zeros((8 - 1 - DEC_BATCH, D_MODEL), F32)], axis=0)
    mod = _modulation(c_all, w_mod[l], row(b_mod[l])).reshape(8, 6, D_MODEL)

    xp = x_prompt.reshape(N_CTX, D_MODEL)
    xl = x_sample.reshape(N_LAT, D_MODEL)
    proj, dt_raw = _inproj(xp, xl, mod, w_main, w_dt)
    dt_t, acs_t = _dtprep(dt_raw, bias_col, alog_col)

    qn, kn = row(q_norm[l]), row(k_norm[l])
    attn_c, k_new, v_new = _attn_ctx(proj, qn, kn)
    cos_t, sa_t, sb_t = _rope_tables()
    attn_l = _attn_lat(proj, cache_k[:, l].reshape(DEC_BATCH, PAST_LEN, KV_DIM),
                       cache_v[:, l].reshape(DEC_BATCH, PAST_LEN, KV_DIM), cos_t, sa_t, sb_t, qn, kn)

    cw, cb = conv_w[l], row(conv_b[l])
    y_c, h_new = _ssd(proj, dt_t, acs_t, cw, cb, dskip_row, None, seq=SEQ, nbatch=BATCH,
                      row_blk0=0, emit_state=True)
    h0 = state_ssd[:, l].reshape(DEC_BATCH, 2, H_SSD * SSD_P, SSD_N)
    (y_l,) = _ssd(proj, dt_t, acs_t, cw, cb, dskip_row, h0, seq=DEC_SEQ, nbatch=DEC_BATCH,
                  row_blk0=N_CTX // DEC_SEQ, emit_state=False)

    x1 = _merge(attn_c, attn_l, y_c, y_l, proj, xp, xl, mod, row(ssd_norm[l]), w_attn_o[l].astype(BF16),
                w_ssd_o[l].astype(BF16), w_out[l].astype(BF16), row(ln1_g[l]), row(ln1_b[l]))
    out_p, out_l = _mlp(x1, mod, w_mlp1[l], row(b_mlp1[l]), w_mlp2[l], row(b_mlp2[l]),
                        row(ln2_g[l]), row(ln2_b[l]))

    y_p = out_p.reshape(BATCH, SEQ, D_MODEL)
    y_s = out_l.reshape(DEC_BATCH, DEC_SEQ, D_MODEL)
    new_cache_k = k_new.reshape(BATCH, DEPTH, SEQ, H_KV, HD)
    new_cache_v = v_new.reshape(BATCH, DEPTH, SEQ, H_KV, HD)
    new_state = h_new.reshape(BATCH, DEPTH, 2, H_SSD, SSD_P, SSD_N)
    return (y_p, y_s, new_cache_k, new_cache_v, new_state)
```

```python
import functools

import jax
import jax.numpy as jnp
import numpy as np
from jax import lax
from jax.experimental import pallas as pl
from jax.experimental.pallas import tpu as pltpu

D_MODEL = 1024
BATCH = 16
SEQ = 256
DEC_BATCH = 4
DEC_SEQ = 1024
PAST_LEN = 512
GRID_W = 64
H_Q = 8
H_KV = 2
HD = 128
ROPE_F = HD // 4
ROPE_THETA = 10000.0
D_INNER = 2 * D_MODEL
SSD_P = 64
H_SSD = D_INNER // SSD_P
SSD_G = 8
SSD_R = H_SSD // SSD_G
SSD_N = 128
CONV_K = 5
CHUNK = 128
D_FF = 4 * D_MODEL
DEPTH = 1
ALPHA = (2.0 * DEPTH) ** 0.25
EPS = 1e-6
LOG2E = 1.4426950408889634
Q_DIM = H_Q * HD
KV_DIM = H_KV * HD
BC_DIM = SSD_G * SSD_N
XBC_DIM = D_INNER + 2 * BC_DIM
DT_DIM = 2 * H_SSD
REP = H_Q // H_KV

N_CTX = BATCH * SEQ
N_LAT = DEC_BATCH * DEC_SEQ
N_TOK = N_CTX + N_LAT

LANE = 128

COL_Z = 0
COL_GATE = COL_Z + D_INNER
COL_X = COL_GATE + 2 * D_MODEL
COL_B = COL_X + D_INNER
COL_C = COL_B + BC_DIM
COL_Q = COL_C + BC_DIM
COL_K = COL_Q + Q_DIM
COL_V = COL_K + KV_DIM
COL_DT = COL_V + KV_DIM
INPROJ_TN = 2560
PROJ_W = -(-COL_DT // INPROJ_TN) * INPROJ_TN

VMEM_LIMIT = 52 * 1024 * 1024
ATTN_HEADS_PER_CHAIN = 1

F32 = jnp.float32
BF16 = jnp.bfloat16


def _ln(x):
    mu = jnp.mean(x, axis=-1, keepdims=True)
    xc = x - mu
    var = jnp.mean(xc * xc, axis=-1, keepdims=True)
    return xc * lax.rsqrt(var + EPS)


def _rms(x, g):
    return x * lax.rsqrt(jnp.mean(x * x, axis=-1, keepdims=True) + EPS) * g


def _mod_row(i, tm):
    nct = N_CTX // tm
    tpl = DEC_SEQ // tm
    return jnp.where(i < nct, 0, 1 + jnp.maximum(i - nct, 0) // tpl)


def _ctx_blk(i, tm):
    return jnp.minimum(i, N_CTX // tm - 1)


def _lat_blk(i, tm):
    return jnp.maximum(i - N_CTX // tm, 0)


def _params(sem):
    return pltpu.CompilerParams(dimension_semantics=sem, vmem_limit_bytes=VMEM_LIMIT)


def _mod_kernel(c_ref, w_ref, b_ref, o_ref):
    c = c_ref[...]
    s = c * jax.nn.sigmoid(c)
    o_ref[...] = jnp.dot(s.astype(BF16), w_ref[...].astype(BF16), preferred_element_type=F32) + b_ref[...]


def _modulation(c_all, w_mod, b_mod):
    tn = 1536
    return pl.pallas_call(
        _mod_kernel,
        grid=(6 * D_MODEL // tn,),
        in_specs=[pl.BlockSpec((8, D_MODEL), lambda j: (0, 0)),
                  pl.BlockSpec((D_MODEL, tn), lambda j: (0, j)),
                  pl.BlockSpec((1, tn), lambda j: (0, j))],
        out_specs=pl.BlockSpec((8, tn), lambda j: (0, j)),
        out_shape=jax.ShapeDtypeStruct((8, 6 * D_MODEL), F32),
        compiler_params=_params(("arbitrary",)),
        name="modulation",
    )(c_all, w_mod, b_mod)


def _inproj_kernel(xp_ref, xl_ref, mod_ref, w_ref, wdt_ref, o_ref, dt_ref, h_scr, *, tm):
    @pl.when(pl.program_id(1) == 0)
    def _():
        x = jnp.where(pl.program_id(0) < N_CTX // tm, xp_ref[...], xl_ref[...])
        h = _ln(x) * (1.0 + mod_ref[1:2, :]) + mod_ref[0:1, :]
        h_scr[...] = h.astype(BF16)
        dt_ref[...] = jnp.dot(h_scr[...], wdt_ref[...], preferred_element_type=F32)

    o_ref[...] = jnp.dot(h_scr[...], w_ref[...], preferred_element_type=F32).astype(BF16)


def _inproj(xp, xl, mod, w_main, w_dt):
    tm, tn = 1024, INPROJ_TN
    return pl.pallas_call(
        functools.partial(_inproj_kernel, tm=tm),
        grid=(N_TOK // tm, PROJ_W // tn),
        in_specs=[pl.BlockSpec((tm, D_MODEL), lambda i, j: (_ctx_blk(i, tm), 0)),
                  pl.BlockSpec((tm, D_MODEL), lambda i, j: (_lat_blk(i, tm), 0)),
                  pl.BlockSpec((None, 6, D_MODEL), lambda i, j: (_mod_row(i, tm), 0, 0)),
                  pl.BlockSpec((D_MODEL, tn), lambda i, j: (0, j)),
                  pl.BlockSpec((D_MODEL, LANE), lambda i, j: (0, 0))],
        out_specs=[pl.BlockSpec((tm, tn), lambda i, j: (i, j)),
                   pl.BlockSpec((tm, LANE), lambda i, j: (i, 0))],
        out_shape=[jax.ShapeDtypeStruct((N_TOK, PROJ_W), BF16),
                   jax.ShapeDtypeStruct((N_TOK, LANE), F32)],
        scratch_shapes=[pltpu.VMEM((tm, D_MODEL), BF16)],
        compiler_params=_params(("arbitrary", "arbitrary")),
        name="inproj",
    )(xp, xl, mod, w_main, w_dt)


def _dtprep_kernel(p_ref, bias_ref, alog_ref, dt_ref, acs_ref, *, nchunk):
    a_head = -jnp.exp(alog_ref[...]) * LOG2E
    ii = lax.broadcasted_iota(jnp.int32, (CHUNK, CHUNK), 0)
    jj = lax.broadcasted_iota(jnp.int32, (CHUNK, CHUNK), 1)
    upper = (ii <= jj).astype(F32)
    lower = (ii >= jj).astype(F32)
    is_fwd = (ii & 7) < SSD_R
    for c in range(nchunk):
        t = p_ref[c * CHUNK:(c + 1) * CHUNK, :]
        tt = t.T + bias_ref[...]
        dtv = jnp.maximum(tt, 0.0) + jnp.log1p(jnp.exp(-jnp.abs(tt)))
        a = dtv * a_head
        pre = jnp.dot(a, upper, precision=lax.Precision.HIGHEST, preferred_element_type=F32)
        suf = jnp.dot(a, lower, precision=lax.Precision.HIGHEST, preferred_element_type=F32)
        dt_ref[c] = dtv
        acs_ref[c] = jnp.where(is_fwd, pre, suf)


def _dtprep(proj, bias_col, alog_col):
    nchunk = 4
    tm = nchunk * CHUNK
    nct = N_TOK // CHUNK
    out = jax.ShapeDtypeStruct((nct, LANE, CHUNK), F32)
    return pl.pallas_call(
        functools.partial(_dtprep_kernel, nchunk=nchunk),
        grid=(N_TOK // tm,),
        in_specs=[pl.BlockSpec((tm, LANE), lambda i: (i, 0)),
                  pl.BlockSpec((LANE, 1), lambda i: (0, 0)),
                  pl.BlockSpec((LANE, 1), lambda i: (0, 0))],
        out_specs=[pl.BlockSpec((nchunk, LANE, CHUNK), lambda i: (i, 0, 0)),
                   pl.BlockSpec((nchunk, LANE, CHUNK), lambda i: (i, 0, 0))],
        out_shape=[out, out],
        compiler_params=_params(("arbitrary",)),
        name="dtprep",
    )(proj, bias_col, alog_col)


def _rope(x, cos, sa, sb):
    return x * cos + pltpu.roll(x, HD - ROPE_F, axis=1) * sa + pltpu.roll(x, ROPE_F, axis=1) * sb


def _softmax_pv(q4, k_b, v_b):
    s = lax.dot_general(q4.astype(BF16), k_b, (((1,), (1,)), ((), ())), preferred_element_type=F32)
    m = jnp.max(s, axis=-1, keepdims=True)
    p = jnp.exp2(s - m)
    l = jnp.sum(p, axis=-1, keepdims=True)
    o = jnp.dot(p.astype(BF16), v_b, preferred_element_type=F32)
    return o / l


def _attn_ctx_kernel(q_ref, k_ref, v_ref, qn_ref, kn_ref, o_ref, ko_ref, vo_ref):
    k = _rms(k_ref[...].astype(F32), kn_ref[...])
    v = v_ref[...]
    ko_ref[...] = k
    vo_ref[...] = v.astype(F32)
    q = q_ref[...].astype(F32)
    qg = qn_ref[...] * (LOG2E * HD ** -0.5)
    q4 = jnp.concatenate([_rms(q[:, h * HD:(h + 1) * HD], qg) for h in range(REP)], axis=0)
    o = _softmax_pv(q4, k.astype(BF16), v)
    for h in range(REP):
        o_ref[:, h * HD:(h + 1) * HD] = o[h * SEQ:(h + 1) * SEQ].astype(BF16)


def _attn_ctx(proj, q_norm, k_norm):
    qw = REP * HD
    kv_out = jax.ShapeDtypeStruct((BATCH, SEQ, KV_DIM), F32)
    return pl.pallas_call(
        _attn_ctx_kernel,
        grid=(BATCH, H_KV),
        in_specs=[pl.BlockSpec((SEQ, qw), lambda b, g: (b, COL_Q // qw + g)),
                  pl.BlockSpec((SEQ, HD), lambda b, g: (b, COL_K // HD + g)),
                  pl.BlockSpec((SEQ, HD), lambda b, g: (b, COL_V // HD + g)),
                  pl.BlockSpec((1, HD), lambda b, g: (0, 0)),
                  pl.BlockSpec((1, HD), lambda b, g: (0, 0))],
        out_specs=[pl.BlockSpec((SEQ, qw), lambda b, g: (b, g)),
                   pl.BlockSpec((None, SEQ, HD), lambda b, g: (b, 0, g)),
                   pl.BlockSpec((None, SEQ, HD), lambda b, g: (b, 0, g))],
        out_shape=[jax.ShapeDtypeStruct((N_CTX, Q_DIM), BF16), kv_out, kv_out],
        compiler_params=_params(("arbitrary", "arbitrary")),
        name="attn_ctx",
    )(proj, proj, proj, q_norm, k_norm)


def _attn_lat_kernel(q_ref, k_ref, v_ref, ck_ref, cv_ref, cos_ref, sa_ref, sb_ref, qn_ref, kn_ref,
                     o_ref, kall, vall, *, tq):
    qi = pl.program_id(2)

    @pl.when(qi == 0)
    def _():
        k = _rope(_rms(k_ref[...].astype(F32), kn_ref[...]), cos_ref[...], sa_ref[...], sb_ref[...])
        kall[0:PAST_LEN, :] = ck_ref[...].astype(BF16)
        kall[PAST_LEN:, :] = k.astype(BF16)
        vall[0:PAST_LEN, :] = cv_ref[...].astype(BF16)
        vall[PAST_LEN:, :] = v_ref[...]

    r0 = pl.multiple_of(qi * tq, tq)
    cos = cos_ref[pl.ds(r0, tq), :]
    sa = sa_ref[pl.ds(r0, tq), :]
    sb = sb_ref[pl.ds(r0, tq), :]
    qg = qn_ref[...] * (LOG2E * HD ** -0.5)
    for h0 in range(0, REP, ATTN_HEADS_PER_CHAIN):
        hs = range(h0, h0 + ATTN_HEADS_PER_CHAIN)
        qh = jnp.concatenate(
            [_rope(_rms(q_ref[:, h * HD:(h + 1) * HD].astype(F32), qg), cos, sa, sb) for h in hs], axis=0)
        o = _softmax_pv(qh, kall[...], vall[...])
        for n, h in enumerate(hs):
            o_ref[:, h * HD:(h + 1) * HD] = o[n * tq:(n + 1) * tq].astype(BF16)


def _attn_lat(proj, cache_k, cache_v, cos, sa, sb, q_norm, k_norm):
    tq = 256
    qw = REP * HD
    nq = DEC_SEQ // tq
    row0 = N_CTX // tq
    tab = pl.BlockSpec((DEC_SEQ, HD), lambda b, g, i: (0, 0))
    return pl.pallas_call(
        functools.partial(_attn_lat_kernel, tq=tq),
        grid=(DEC_BATCH, H_KV, nq),
        in_specs=[pl.BlockSpec((tq, qw), lambda b, g, i: (row0 + b * nq + i, COL_Q // qw + g)),
                  pl.BlockSpec((DEC_SEQ, HD), lambda b, g, i: (N_CTX // DEC_SEQ + b, COL_K // HD + g)),
                  pl.BlockSpec((DEC_SEQ, HD), lambda b, g, i: (N_CTX // DEC_SEQ + b, COL_V // HD + g)),
                  pl.BlockSpec((None, PAST_LEN, HD), lambda b, g, i: (b, 0, g)),
                  pl.BlockSpec((None, PAST_LEN, HD), lambda b, g, i: (b, 0, g)),
                  tab, tab, tab,
                  pl.BlockSpec((1, HD), lambda b, g, i: (0, 0)),
                  pl.BlockSpec((1, HD), lambda b, g, i: (0, 0))],
        out_specs=pl.BlockSpec((tq, qw), lambda b, g, i: (b * nq + i, g)),
        out_shape=jax.ShapeDtypeStruct((N_LAT, Q_DIM), BF16),
        scratch_shapes=[pltpu.VMEM((PAST_LEN + DEC_SEQ, HD), BF16),
                        pltpu.VMEM((PAST_LEN + DEC_SEQ, HD), BF16)],
        compiler_params=_params(("arbitrary", "arbitrary", "arbitrary")),
        name="attn_lat",
    )(proj, proj, proj, cache_k, cache_v, cos, sa, sb, q_norm, k_norm)


def _ssd_kernel(*refs, seq, has_h0, emit_state):
    (x_ref, b_ref, c_ref, dt_ref, acs_ref, cwx_ref, cwb_ref, cwc_ref,
     cbx_ref, cbb_ref, cbc_ref, dsk_ref) = refs[:12]
    pos = 12
    h0_ref = None
    if has_h0:
        h0_ref = refs[pos]
        pos += 1
    y_ref = refs[pos]
    pos += 1
    hout_ref = None
    if emit_state:
        hout_ref = refs[pos]
        pos += 1
    pad, xst, btf, cbf, csb, acc, ht, yacc = refs[pos:]
    nc = seq // CHUNK
    half = SSD_R * SSD_P // 2
    halo = 8

    ii = lax.broadcasted_iota(jnp.int32, (CHUNK, CHUNK), 0)
    jj = lax.broadcasted_iota(jnp.int32, (CHUNK, CHUNK), 1)
    lane_lo = jj < SSD_P

    def pair(a, b):
        return jnp.where(lane_lo, a, b)

    pad[0:halo, :] = jnp.zeros((halo, 4 * LANE), F32)
    pad[halo + seq:, :] = jnp.zeros((halo, 4 * LANE), F32)
    pad[halo:halo + seq, 0:2 * LANE] = x_ref[...].astype(F32)
    pad[halo:halo + seq, 2 * LANE:3 * LANE] = b_ref[...].astype(F32)
    pad[halo:halo + seq, 3 * LANE:] = c_ref[...].astype(F32)

    segs = [(cwx_ref, cbx_ref, 0), (cwx_ref, cbx_ref, LANE), (cwb_ref, cbb_ref, 0), (cwc_ref, cbc_ref, 0)]
    for rb in range(nc):
        vals = []
        for blk, (w_ref, bias_ref, lo) in enumerate(segs):
            a = None
            for k in range(CONV_K):
                s = halo + rb * CHUNK + k - CONV_K // 2
                term = pad[s:s + CHUNK, blk * LANE:(blk + 1) * LANE] * w_ref[k:k + 1, lo:lo + LANE]
                a = term if a is None else a + term
            a = a + bias_ref[:, lo:lo + LANE]
            vals.append(a * jax.nn.sigmoid(a))
        for p in range(2):
            v = vals[p]
            yacc[rb * CHUNK:(rb + 1) * CHUNK, p * LANE:(p + 1) * LANE] = v * dsk_ref[:, p * LANE:(p + 1) * LANE]
            vh = v.astype(BF16)
            zh = jnp.zeros_like(vh)
            xst[rb, p, 0:CHUNK, :] = jnp.where(lane_lo, vh, zh)
            xst[rb, p, CHUNK:, :] = jnp.where(lane_lo, zh, vh)
        vb = vals[2]
        vc = vals[3].astype(BF16)
        btf[rb] = vb.T
        csb[rb] = vc
        cbf[rb] = lax.dot_general(vc, vb.astype(BF16), (((1,), (1,)), ((), ())), preferred_element_type=F32)
        acc[rb] = jnp.concatenate([acs_ref[rb]] * (CHUNK // 8), axis=0).T

    for d in range(2):
        if has_h0:
            ht[d, :, 0:half] = h0_ref[d, 0:half, :].T
            ht[d, :, half:] = h0_ref[d, half:, :].T
        else:
            ht[d] = jnp.zeros((SSD_N, 2 * half), F32)

    def chunk(c, d):
        r0 = pl.multiple_of(c * CHUNK, CHUNK)
        cbm = cbf[c]
        bt = btf[c]
        cc = csb[c]
        acol = acc[c]
        arow = acs_ref[c]
        drow = dt_ref[c]
        h_prev = ht[d]
        yoff = jnp.dot(cc, h_prev.astype(BF16), preferred_element_type=F32)
        mask = (ii >= jj) if d == 0 else (ii <= jj)
        cbm = jnp.where(mask, cbm, 0.0)
        last = CHUNK - 1 if d == 0 else 0
        ms, bws, eas, cds = [], [], [], []
        for r in range(SSD_R):
            col = d * SSD_R + r
            acb = jnp.broadcast_to(acol[:, col:col + 1], (CHUNK, CHUNK))
            ar = arow[col:col + 1, :]
            dr = drow[col:col + 1, :]
            dec = jnp.exp2(jnp.minimum(acb - ar, 0.0))
            ms.append((cbm * dec * dr).astype(BF16))
            eas.append(jnp.exp2(acb))
            a_last = ar[:, last:last + 1]
            bws.append((bt * (dr * jnp.exp2(a_last - ar))).astype(BF16))
            cds.append(jnp.exp2(a_last))
        ys, sts, cdm = [], [], []
        for p in range(2):
            r_a, r_b = 2 * p, 2 * p + 1
            x2 = xst[c, p]
            yd = jnp.dot(jnp.concatenate([ms[r_a], ms[r_b]], axis=1), x2, preferred_element_type=F32)
            ys.append(yd + yoff[:, p * LANE:(p + 1) * LANE] * pair(eas[r_a], eas[r_b]))
            sts.append(jnp.dot(jnp.concatenate([bws[r_a], bws[r_b]], axis=1), x2, preferred_element_type=F32))
            cdm.append(pair(cds[r_a], cds[r_b]))
        ht[d] = h_prev * jnp.concatenate(cdm, axis=1) + jnp.concatenate(sts, axis=1)
        yacc[pl.ds(r0, CHUNK), :] += jnp.concatenate(ys, axis=1)

    def body(t, carry):
        chunk(t, 0)
        chunk(nc - 1 - t, 1)
        return carry

    lax.fori_loop(0, nc, body, 0, unroll=True)
    y_ref[...] = yacc[...].astype(BF16)

    if emit_state:
        for d in range(2):
            hout_ref[d, 0:half, :] = ht[d, :, 0:half].T
            hout_ref[d, half:, :] = ht[d, :, half:].T


def _ssd(proj, dt_t, acs_t, conv_w, conv_b, dskip, h0, *, seq, nbatch, row_blk0, emit_state):
    gw = SSD_R * SSD_P
    nc = seq // CHUNK
    has_h0 = h0 is not None
    xb0, bb0, cb0 = COL_X // gw, COL_B // SSD_N, COL_C // SSD_N
    in_specs = [
        pl.BlockSpec((seq, gw), lambda b, g: (row_blk0 + b, xb0 + g)),
        pl.BlockSpec((seq, SSD_N), lambda b, g: (row_blk0 + b, bb0 + g)),
        pl.BlockSpec((seq, SSD_N), lambda b, g: (row_blk0 + b, cb0 + g)),
        pl.BlockSpec((nc, 8, CHUNK), lambda b, g: (row_blk0 + b, g, 0)),
        pl.BlockSpec((nc, 8, CHUNK), lambda b, g: (row_blk0 + b, g, 0)),
        pl.BlockSpec((CONV_K, gw), lambda b, g: (0, g)),
        pl.BlockSpec((CONV_K, SSD_N), lambda b, g: (0, D_INNER // SSD_N + g)),
        pl.BlockSpec((CONV_K, SSD_N), lambda b, g: (0, (D_INNER + BC_DIM) // SSD_N + g)),
        pl.BlockSpec((1, gw), lambda b, g: (0, g)),
        pl.BlockSpec((1, SSD_N), lambda b, g: (0, D_INNER // SSD_N + g)),
        pl.BlockSpec((1, SSD_N), lambda b, g: (0, (D_INNER + BC_DIM) // SSD_N + g)),
        pl.BlockSpec((1, gw), lambda b, g: (0, g)),
    ]
    args = [proj, proj, proj, dt_t, acs_t, conv_w, conv_w, conv_w, conv_b, conv_b, conv_b, dskip]
    state_spec = pl.BlockSpec((None, 2, gw, SSD_N), lambda b, g: (b, 0, g, 0))
    if has_h0:
        in_specs.append(state_spec)
        args.append(h0)
    out_specs = [pl.BlockSpec((seq, gw), lambda b, g: (b, g))]
    out_shape = [jax.ShapeDtypeStruct((nbatch * seq, D_INNER), BF16)]
    if emit_state:
        out_specs.append(state_spec)
        out_shape.append(jax.ShapeDtypeStruct((nbatch, 2, H_SSD * SSD_P, SSD_N), F32))
    return pl.pallas_call(
        functools.partial(_ssd_kernel, seq=seq, has_h0=has_h0, emit_state=emit_state),
        grid=(nbatch, SSD_G),
        in_specs=in_specs,
        out_specs=out_specs,
        out_shape=out_shape,
        scratch_shapes=[pltpu.VMEM((seq + 16, 4 * LANE), F32),
                        pltpu.VMEM((nc, 2, 2 * CHUNK, LANE), BF16),
                        pltpu.VMEM((nc, SSD_N, CHUNK), F32),
                        pltpu.VMEM((nc, CHUNK, CHUNK), F32),
                        pltpu.VMEM((nc, CHUNK, SSD_N), BF16),
                        pltpu.VMEM((nc, CHUNK, LANE), F32),
                        pltpu.VMEM((2, SSD_N, gw), F32),
                        pltpu.VMEM((seq, gw), F32)],
        compiler_params=_params(("arbitrary", "arbitrary")),
        name="ssd_lat" if has_h0 else "ssd_ctx",
    )(*args)


def _merge_kernel(ac_ref, al_ref, yc_ref, yl_ref, z_ref, g_ref, xp_ref, xl_ref, mod_ref, nrm_ref,
                  wa_ref, ws_ref, wo_ref, l1g_ref, l1b_ref, o_ref, *, tm):
    is_ctx = pl.program_id(0) < N_CTX // tm
    attn = jnp.where(is_ctx, ac_ref[...], al_ref[...])
    y = jnp.where(is_ctx, yc_ref[...], yl_ref[...])
    x = jnp.where(is_ctx, xp_ref[...], xl_ref[...])
    z = z_ref[...].astype(F32)
    yg = y.astype(F32) * (z * jax.nn.sigmoid(z))
    yn = yg * lax.rsqrt(jnp.mean(yg * yg, axis=-1, keepdims=True) + EPS) * nrm_ref[...]
    pa = jnp.dot(attn, wa_ref[...], preferred_element_type=F32)
    ps = jnp.dot(yn.astype(BF16), ws_ref[...], preferred_element_type=F32)
    g = g_ref[...].astype(F32)
    merged = jax.nn.sigmoid(g[:, :D_MODEL]) * pa + jax.nn.sigmoid(g[:, D_MODEL:]) * ps
    mix = jnp.dot(merged.astype(BF16), wo_ref[...], preferred_element_type=F32)
    x1 = _ln(ALPHA * x + mod_ref[2:3, :] * mix)
    o_ref[...] = x1 * l1g_ref[...] + l1b_ref[...]


def _merge(attn_c, attn_l, y_c, y_l, proj, xp, xl, mod, ssd_norm, wa, ws, wo, l1g, l1b):
    tm = 256
    const = lambda shape: pl.BlockSpec(shape, lambda i: (0, 0))
    ctx_rows = lambda w: pl.BlockSpec((tm, w), lambda i: (_ctx_blk(i, tm), 0))
    lat_rows = lambda w: pl.BlockSpec((tm, w), lambda i: (_lat_blk(i, tm), 0))
    return pl.pallas_call(
        functools.partial(_merge_kernel, tm=tm),
        grid=(N_TOK // tm,),
        in_specs=[ctx_rows(Q_DIM), lat_rows(Q_DIM), ctx_rows(D_INNER), lat_rows(D_INNER),
                  pl.BlockSpec((tm, D_INNER), lambda i: (i, COL_Z // D_INNER)),
                  pl.BlockSpec((tm, 2 * D_MODEL), lambda i: (i, COL_GATE // (2 * D_MODEL))),
                  pl.BlockSpec((tm, D_MODEL), lambda i: (_ctx_blk(i, tm), 0)),
                  pl.BlockSpec((tm, D_MODEL), lambda i: (_lat_blk(i, tm), 0)),
                  pl.BlockSpec((None, 6, D_MODEL), lambda i: (_mod_row(i, tm), 0, 0)),
                  const((1, D_INNER)), const((Q_DIM, D_MODEL)), const((D_INNER, D_MODEL)),
                  const((D_MODEL, D_MODEL)), const((1, D_MODEL)), const((1, D_MODEL))],
        out_specs=pl.BlockSpec((tm, D_MODEL), lambda i: (i, 0)),
        out_shape=jax.ShapeDtypeStruct((N_TOK, D_MODEL), F32),
        compiler_params=_params(("arbitrary",)),
        name="merge",
    )(attn_c, attn_l, y_c, y_l, proj, proj, xp, xl, mod, ssd_norm, wa, ws, wo, l1g, l1b)


def _mlp_kernel(x1_ref, mod_ref, w1_ref, b1_ref, w2_ref, b2_ref, l2g_ref, l2b_ref, op_ref, ol_ref,
                h_scr, acc_scr, *, tm):
    i = pl.program_id(0)
    j = pl.program_id(1)

    @pl.when(j == 0)
    def _():
        h = _ln(x1_ref[...]) * (1.0 + mod_ref[4:5, :]) + mod_ref[3:4, :]
        h_scr[...] = h.astype(BF16)
        acc_scr[...] = jnp.zeros_like(acc_scr)

    u = jnp.dot(h_scr[...], w1_ref[...].astype(BF16), preferred_element_type=F32) + b1_ref[...]
    u = jnp.maximum(u, 0.0)
    acc_scr[...] += jnp.dot((u * u).astype(BF16), w2_ref[...].astype(BF16), preferred_element_type=F32)

    def result():
        ff = acc_scr[...] + b2_ref[...]
        x2 = _ln(ALPHA * x1_ref[...] + mod_ref[5:6, :] * ff)
        return x2 * l2g_ref[...] + l2b_ref[...]

    last = j == pl.num_programs(1) - 1
    is_ctx = i < N_CTX // tm

    @pl.when(last & is_ctx)
    def _():
        op_ref[...] = result()

    @pl.when(last & jnp.logical_not(is_ctx))
    def _():
        ol_ref[...] = result()


def _mlp(x1, mod, w1, b1, w2, b2, l2g, l2b):
    tm, tf = 1024, 512
    return pl.pallas_call(
        functools.partial(_mlp_kernel, tm=tm),
        grid=(N_TOK // tm, D_FF // tf),
        in_specs=[pl.BlockSpec((tm, D_MODEL), lambda i, j: (i, 0)),
                  pl.BlockSpec((None, 6, D_MODEL), lambda i, j: (_mod_row(i, tm), 0, 0)),
                  pl.BlockSpec((D_MODEL, tf), lambda i, j: (0, j)),
                  pl.BlockSpec((1, tf), lambda i, j: (0, j)),
                  pl.BlockSpec((tf, D_MODEL), lambda i, j: (j, 0)),
                  pl.BlockSpec((1, D_MODEL), lambda i, j: (0, 0)),
                  pl.BlockSpec((1, D_MODEL), lambda i, j: (0, 0)),
                  pl.BlockSpec((1, D_MODEL), lambda i, j: (0, 0))],
        out_specs=[pl.BlockSpec((tm, D_MODEL), lambda i, j: (_ctx_blk(i, tm), 0)),
                   pl.BlockSpec((tm, D_MODEL), lambda i, j: (_lat_blk(i, tm), 0))],
        out_shape=[jax.ShapeDtypeStruct((N_CTX, D_MODEL), F32), jax.ShapeDtypeStruct((N_LAT, D_MODEL), F32)],
        scratch_shapes=[pltpu.VMEM((tm, D_MODEL), BF16), pltpu.VMEM((tm, D_MODEL), F32)],
        compiler_params=_params(("arbitrary", "arbitrary")),
        name="mlp",
    )(x1, mod, w1, b1, w2, b2, l2g, l2b)


def _dt_perm():
    idx = []
    for n in range(DT_DIM):
        g, d, r = n // 8, (n % 8) // SSD_R, n % SSD_R
        idx.append(d * H_SSD + g * SSD_R + r)
    return jnp.asarray(idx, dtype=jnp.int32)


def _rope_tables():
    t = np.arange(DEC_SEQ)
    posn = np.stack([t // GRID_W, t % GRID_W], axis=-1).astype(np.float32)
    inv = (ROPE_THETA ** (-np.arange(ROPE_F, dtype=np.float32) / ROPE_F)).astype(np.float32)
    ang = posn[:, :, None] * inv
    cos, sin = np.cos(ang), np.sin(ang)
    zero = np.zeros_like(sin[:, 0])
    cos_t = np.concatenate([cos[:, 0], cos[:, 0], cos[:, 1], cos[:, 1]], axis=-1)
    sa_t = np.concatenate([-sin[:, 0], zero, -sin[:, 1], zero], axis=-1)
    sb_t = np.concatenate([zero, sin[:, 0], zero, sin[:, 1]], axis=-1)
    return tuple(jnp.asarray(a, dtype=F32) for a in (cos_t, sa_t, sb_t))


def kernel(x_prompt, x_sample, cache_k, cache_v, state_ssd, c, c_ctx, w_mod, b_mod, w_in, q_norm, k_norm, conv_w, conv_b, a_log, dt_bias, d_skip, ssd_norm, w_attn_o, w_ssd_o, w_out, ln1_g, ln1_b, w_mlp1, b_mlp1, w_mlp2, b_mlp2, ln2_g, ln2_b):
    l = 0
    perm = _dt_perm()
    w = w_in[l]
    o_q, o_k, o_v, o_z = 0, Q_DIM, Q_DIM + KV_DIM, Q_DIM + 2 * KV_DIM
    o_xbc = o_z + D_INNER
    o_dt = o_xbc + XBC_DIM
    o_gate = o_dt + DT_DIM
    w_dt = jnp.concatenate([jnp.take(w[:, o_dt:o_dt + DT_DIM], perm, axis=1),
                            jnp.zeros((D_MODEL, LANE - DT_DIM), F32)], axis=1).astype(BF16)
    w_main = jnp.concatenate(
        [w[:, o_z:o_z + D_INNER], w[:, o_gate:o_gate + 2 * D_MODEL], w[:, o_xbc:o_xbc + XBC_DIM],
         w[:, o_q:o_q + Q_DIM], w[:, o_k:o_k + KV_DIM], w[:, o_v:o_v + KV_DIM],
         jnp.zeros((D_MODEL, PROJ_W - COL_DT), F32)], axis=1).astype(BF16)
    pad = jnp.zeros((LANE - DT_DIM,), F32)
    bias_col = jnp.concatenate([jnp.take(dt_bias[l].reshape(-1), perm), pad]).reshape(LANE, 1)
    alog_col = jnp.concatenate([jnp.take(a_log[l].reshape(-1), perm), pad]).reshape(LANE, 1)
    dskip_row = jnp.repeat(d_skip[l], SSD_P).reshape(1, D_INNER)
    row = lambda v: v.reshape(1, -1)

    c_all = jnp.concatenate([c_ctx[None, :], c, jnp.zeros((8 - 1 - DEC_BATCH, D_MODEL), F32)], axis=0)
    mod = _modulation(c_all, w_mod[l], row(b_mod[l])).reshape(8, 6, D_MODEL)

    xp = x_prompt.reshape(N_CTX, D_MODEL)
    xl = x_sample.reshape(N_LAT, D_MODEL)
    proj, dt_raw = _inproj(xp, xl, mod, w_main, w_dt)
    dt_t, acs_t = _dtprep(dt_raw, bias_col, alog_col)

    qn, kn = row(q_norm[l]), row(k_norm[l])
    attn_c, k_new, v_new = _attn_ctx(proj, qn, kn)
    cos_t, sa_t, sb_t = _rope_tables()
    attn_l = _attn_lat(proj, cache_k[:, l].reshape(DEC_BATCH, PAST_LEN, KV_DIM),
                       cache_v[:, l].reshape(DEC_BATCH, PAST_LEN, KV_DIM), cos_t, sa_t, sb_t, qn, kn)

    cw, cb = conv_w[l], row(conv_b[l])
    y_c, h_new = _ssd(proj, dt_t, acs_t, cw, cb, dskip_row, None, seq=SEQ, nbatch=BATCH,
                      row_blk0=0, emit_state=True)
    h0 = state_ssd[:, l].reshape(DEC_BATCH, 2, H_SSD * SSD_P, SSD_N)
    (y_l,) = _ssd(proj, dt_t, acs_t, cw, cb, dskip_row, h0, seq=DEC_SEQ, nbatch=DEC_BATCH,
                  row_blk0=N_CTX // DEC_SEQ, emit_state=False)

    x1 = _merge(attn_c, attn_l, y_c, y_l, proj, xp, xl, mod, row(ssd_norm[l]), w_attn_o[l].astype(BF16),
                w_ssd_o[l].astype(BF16), w_out[l].astype(BF16), row(ln1_g[l]), row(ln1_b[l]))
    out_p, out_l = _mlp(x1, mod, w_mlp1[l], row(b_mlp1[l]), w_mlp2[l], row(b_mlp2[l]),
                        row(ln2_g[l]), row(ln2_b[l]))

    y_p = out_p.reshape(BATCH, SEQ, D_MODEL)
    y_s = out_l.reshape(DEC_BATCH, DEC_SEQ, D_MODEL)
    new_cache_k = k_new.reshape(BATCH, DEPTH, SEQ, H_KV, HD)
    new_cache_v = v_new.reshape(BATCH, DEPTH, SEQ, H_KV, HD)
    new_state = h_new.reshape(BATCH, DEPTH, 2, H_SSD, SSD_P, SSD_N)
    return (y_p, y_s, new_cache_k, new_cache_v, new_state)
```

```python
import functools

import jax
import jax.numpy as jnp
import numpy as np
from jax import lax
from jax.experimental import pallas as pl
from jax.experimental.pallas import tpu as pltpu

D_MODEL = 1024
BATCH = 16
SEQ = 256
DEC_BATCH = 4
DEC_SEQ = 1024
PAST_LEN = 512
GRID_W = 64
H_Q = 8
H_KV = 2
HD = 128
ROPE_F = HD // 4
ROPE_THETA = 10000.0
D_INNER = 2 * D_MODEL
SSD_P = 64
H_SSD = D_INNER // SSD_P
SSD_G = 8
SSD_R = H_SSD // SSD_G
SSD_N = 128
CONV_K = 5
CHUNK = 128
D_FF = 4 * D_MODEL
DEPTH = 1
ALPHA = (2.0 * DEPTH) ** 0.25
EPS = 1e-6
LOG2E = 1.4426950408889634
Q_DIM = H_Q * HD
KV_DIM = H_KV * HD
BC_DIM = SSD_G * SSD_N
XBC_DIM = D_INNER + 2 * BC_DIM
DT_DIM = 2 * H_SSD
REP = H_Q // H_KV

N_CTX = BATCH * SEQ
N_LAT = DEC_BATCH * DEC_SEQ
N_TOK = N_CTX + N_LAT

LANE = 128

COL_Z = 0
COL_GATE = COL_Z + D_INNER
COL_X = COL_GATE + 2 * D_MODEL
COL_B = COL_X + D_INNER
COL_C = COL_B + BC_DIM
COL_Q = COL_C + BC_DIM
COL_K = COL_Q + Q_DIM
COL_V = COL_K + KV_DIM
COL_DT = COL_V + KV_DIM
INPROJ_TN = 2560
PROJ_W = -(-COL_DT // INPROJ_TN) * INPROJ_TN

VMEM_LIMIT = 52 * 1024 * 1024

F32 = jnp.float32
BF16 = jnp.bfloat16


def _ln(x):
    mu = jnp.mean(x, axis=-1, keepdims=True)
    xc = x - mu
    var = jnp.mean(xc * xc, axis=-1, keepdims=True)
    return xc * lax.rsqrt(var + EPS)


def _rms(x, g):
    return x * lax.rsqrt(jnp.mean(x * x, axis=-1, keepdims=True) + EPS) * g


def _mod_row(i, tm):
    nct = N_CTX // tm
    tpl = DEC_SEQ // tm
    return jnp.where(i < nct, 0, 1 + jnp.maximum(i - nct, 0) // tpl)


def _ctx_blk(i, tm):
    return jnp.minimum(i, N_CTX // tm - 1)


def _lat_blk(i, tm):
    return jnp.maximum(i - N_CTX // tm, 0)


def _params(sem):
    return pltpu.CompilerParams(dimension_semantics=sem, vmem_limit_bytes=VMEM_LIMIT)


def _mod_kernel(c_ref, w_ref, b_ref, o_ref):
    c = c_ref[...]
    s = c * jax.nn.sigmoid(c)
    o_ref[...] = jnp.dot(s.astype(BF16), w_ref[...].astype(BF16), preferred_element_type=F32) + b_ref[...]


def _modulation(c_all, w_mod, b_mod):
    tn = 1536
    return pl.pallas_call(
        _mod_kernel,
        grid=(6 * D_MODEL // tn,),
        in_specs=[pl.BlockSpec((8, D_MODEL), lambda j: (0, 0)),
                  pl.BlockSpec((D_MODEL, tn), lambda j: (0, j)),
                  pl.BlockSpec((1, tn), lambda j: (0, j))],
        out_specs=pl.BlockSpec((8, tn), lambda j: (0, j)),
        out_shape=jax.ShapeDtypeStruct((8, 6 * D_MODEL), F32),
        compiler_params=_params(("arbitrary",)),
        name="modulation",
    )(c_all, w_mod, b_mod)


def _inproj_kernel(xp_ref, xl_ref, mod_ref, w_ref, wdt_ref, o_ref, dt_ref, h_scr, *, tm):
    @pl.when(pl.program_id(1) == 0)
    def _():
        x = jnp.where(pl.program_id(0) < N_CTX // tm, xp_ref[...], xl_ref[...])
        h = _ln(x) * (1.0 + mod_ref[1:2, :]) + mod_ref[0:1, :]
        h_scr[...] = h.astype(BF16)
        dt_ref[...] = jnp.dot(h_scr[...], wdt_ref[...], preferred_element_type=F32)

    o_ref[...] = jnp.dot(h_scr[...], w_ref[...], preferred_element_type=F32).astype(BF16)


def _inproj(xp, xl, mod, w_main, w_dt):
    tm, tn = 1024, INPROJ_TN
    return pl.pallas_call(
        functools.partial(_inproj_kernel, tm=tm),
        grid=(N_TOK // tm, PROJ_W // tn),
        in_specs=[pl.BlockSpec((tm, D_MODEL), lambda i, j: (_ctx_blk(i, tm), 0)),
                  pl.BlockSpec((tm, D_MODEL), lambda i, j: (_lat_blk(i, tm), 0)),
                  pl.BlockSpec((None, 6, D_MODEL), lambda i, j: (_mod_row(i, tm), 0, 0)),
                  pl.BlockSpec((D_MODEL, tn), lambda i, j: (0, j)),
                  pl.BlockSpec((D_MODEL, LANE), lambda i, j: (0, 0))],
        out_specs=[pl.BlockSpec((tm, tn), lambda i, j: (i, j)),
                   pl.BlockSpec((tm, LANE), lambda i, j: (i, 0))],
        out_shape=[jax.ShapeDtypeStruct((N_TOK, PROJ_W), BF16),
                   jax.ShapeDtypeStruct((N_TOK, LANE), F32)],
        scratch_shapes=[pltpu.VMEM((tm, D_MODEL), BF16)],
        compiler_params=_params(("arbitrary", "arbitrary")),
        name="inproj",
    )(xp, xl, mod, w_main, w_dt)


def _dtprep_kernel(p_ref, bias_ref, alog_ref, dt_ref, acs_ref, *, nchunk):
    a_head = -jnp.exp(alog_ref[...]) * LOG2E
    ii = lax.broadcasted_iota(jnp.int32, (CHUNK, CHUNK), 0)
    jj = lax.broadcasted_iota(jnp.int32, (CHUNK, CHUNK), 1)
    upper = (ii <= jj).astype(F32)
    lower = (ii >= jj).astype(F32)
    is_fwd = (ii & 7) < SSD_R
    for c in range(nchunk):
        t = p_ref[c * CHUNK:(c + 1) * CHUNK, :]
        tt = t.T + bias_ref[...]
        dtv = jnp.maximum(tt, 0.0) + jnp.log1p(jnp.exp(-jnp.abs(tt)))
        a = dtv * a_head
        pre = jnp.dot(a, upper, precision=lax.Precision.HIGHEST, preferred_element_type=F32)
        suf = jnp.dot(a, lower, precision=lax.Precision.HIGHEST, preferred_element_type=F32)
        dt_ref[c] = dtv
        acs_ref[c] = jnp.where(is_fwd, pre, suf)


def _dtprep(proj, bias_col, alog_col):
    nchunk = 4
    tm = nchunk * CHUNK
    nct = N_TOK // CHUNK
    out = jax.ShapeDtypeStruct((nct, LANE, CHUNK), F32)
    return pl.pallas_call(
        functools.partial(_dtprep_kernel, nchunk=nchunk),
        grid=(N_TOK // tm,),
        in_specs=[pl.BlockSpec((tm, LANE), lambda i: (i, 0)),
                  pl.BlockSpec((LANE, 1), lambda i: (0, 0)),
                  pl.BlockSpec((LANE, 1), lambda i: (0, 0))],
        out_specs=[pl.BlockSpec((nchunk, LANE, CHUNK), lambda i: (i, 0, 0)),
                   pl.BlockSpec((nchunk, LANE, CHUNK), lambda i: (i, 0, 0))],
        out_shape=[out, out],
        compiler_params=_params(("arbitrary",)),
        name="dtprep",
    )(proj, bias_col, alog_col)


def _rope(x, cos, sa, sb):
    return x * cos + pltpu.roll(x, HD - ROPE_F, axis=1) * sa + pltpu.roll(x, ROPE_F, axis=1) * sb


def _softmax_pv(q4, k_b, v_b):
    s = lax.dot_general(q4.astype(BF16), k_b, (((1,), (1,)), ((), ())), preferred_element_type=F32)
    m = jnp.max(s, axis=-1, keepdims=True)
    p = jnp.exp2(s - m)
    l = jnp.sum(p, axis=-1, keepdims=True)
    o = jnp.dot(p.astype(BF16), v_b, preferred_element_type=F32)
    return o / l


def _attn_ctx_kernel(q_ref, k_ref, v_ref, qn_ref, kn_ref, o_ref, ko_ref, vo_ref):
    qg = qn_ref[...] * (LOG2E * HD ** -0.5)
    for g in range(H_KV):
        k = _rms(k_ref[:, g * HD:(g + 1) * HD].astype(F32), kn_ref[...])
        v = v_ref[:, g * HD:(g + 1) * HD]
        ko_ref[:, g, :] = k
        vo_ref[:, g, :] = v.astype(F32)
        q4 = jnp.concatenate(
            [_rms(q_ref[:, (g * REP + h) * HD:(g * REP + h + 1) * HD].astype(F32), qg) for h in range(REP)], axis=0)
        o = _softmax_pv(q4, k.astype(BF16), v)
        for h in range(REP):
            o_ref[:, (g * REP + h) * HD:(g * REP + h + 1) * HD] = o[h * SEQ:(h + 1) * SEQ].astype(BF16)


def _attn_ctx(proj, q_norm, k_norm):
    kv_out = jax.ShapeDtypeStruct((BATCH, DEPTH, SEQ, H_KV, HD), F32)
    kv_spec = pl.BlockSpec((None, None, SEQ, H_KV, HD), lambda b: (b, 0, 0, 0, 0))
    return pl.pallas_call(
        _attn_ctx_kernel,
        grid=(BATCH,),
        in_specs=[pl.BlockSpec((SEQ, Q_DIM), lambda b: (b, COL_Q // Q_DIM)),
                  pl.BlockSpec((SEQ, KV_DIM), lambda b: (b, COL_K // KV_DIM)),
                  pl.BlockSpec((SEQ, KV_DIM), lambda b: (b, COL_V // KV_DIM)),
                  pl.BlockSpec((1, HD), lambda b: (0, 0)),
                  pl.BlockSpec((1, HD), lambda b: (0, 0))],
        out_specs=[pl.BlockSpec((SEQ, Q_DIM), lambda b: (b, 0)), kv_spec, kv_spec],
        out_shape=[jax.ShapeDtypeStruct((N_CTX, Q_DIM), BF16), kv_out, kv_out],
        compiler_params=_params(("arbitrary",)),
        name="attn_ctx",
    )(proj, proj, proj, q_norm, k_norm)


def _attn_lat_kernel(q_ref, k_ref, v_ref, ck_ref, cv_ref, cos_ref, sa_ref, sb_ref, qn_ref, kn_ref,
                     o_ref, kall, vall, *, tq):
    qi = pl.program_id(1)

    @pl.when(qi == 0)
    def _():
        for g in range(H_KV):
            k = _rms(k_ref[:, g * HD:(g + 1) * HD].astype(F32), kn_ref[...])
            k = _rope(k, cos_ref[...], sa_ref[...], sb_ref[...])
            kall[g, 0:PAST_LEN, :] = ck_ref[:, g, :].astype(BF16)
            kall[g, PAST_LEN:, :] = k.astype(BF16)
            vall[g, 0:PAST_LEN, :] = cv_ref[:, g, :].astype(BF16)
            vall[g, PAST_LEN:, :] = v_ref[:, g * HD:(g + 1) * HD]

    r0 = pl.multiple_of(qi * tq, tq)
    cos = cos_ref[pl.ds(r0, tq), :]
    sa = sa_ref[pl.ds(r0, tq), :]
    sb = sb_ref[pl.ds(r0, tq), :]
    qg = qn_ref[...] * (LOG2E * HD ** -0.5)
    for h in range(H_Q):
        g = h // REP
        qh = _rope(_rms(q_ref[:, h * HD:(h + 1) * HD].astype(F32), qg), cos, sa, sb)
        o_ref[:, h * HD:(h + 1) * HD] = _softmax_pv(qh, kall[g], vall[g]).astype(BF16)


def _attn_lat(proj, cache_k, cache_v, cos, sa, sb, q_norm, k_norm):
    tq = 256
    nq = DEC_SEQ // tq
    row0 = N_CTX // tq
    tab = pl.BlockSpec((DEC_SEQ, HD), lambda b, i: (0, 0))
    cache_spec = pl.BlockSpec((None, None, PAST_LEN, H_KV, HD), lambda b, i: (b, 0, 0, 0, 0))
    return pl.pallas_call(
        functools.partial(_attn_lat_kernel, tq=tq),
        grid=(DEC_BATCH, nq),
        in_specs=[pl.BlockSpec((tq, Q_DIM), lambda b, i: (row0 + b * nq + i, COL_Q // Q_DIM)),
                  pl.BlockSpec((DEC_SEQ, KV_DIM), lambda b, i: (N_CTX // DEC_SEQ + b, COL_K // KV_DIM)),
                  pl.BlockSpec((DEC_SEQ, KV_DIM), lambda b, i: (N_CTX // DEC_SEQ + b, COL_V // KV_DIM)),
                  cache_spec, cache_spec,
                  tab, tab, tab,
                  pl.BlockSpec((1, HD), lambda b, i: (0, 0)),
                  pl.BlockSpec((1, HD), lambda b, i: (0, 0))],
        out_specs=pl.BlockSpec((tq, Q_DIM), lambda b, i: (b * nq + i, 0)),
        out_shape=jax.ShapeDtypeStruct((N_LAT, Q_DIM), BF16),
        scratch_shapes=[pltpu.VMEM((H_KV, PAST_LEN + DEC_SEQ, HD), BF16),
                        pltpu.VMEM((H_KV, PAST_LEN + DEC_SEQ, HD), BF16)],
        compiler_params=_params(("arbitrary", "arbitrary")),
        name="attn_lat",
    )(proj, proj, proj, cache_k, cache_v, cos, sa, sb, q_norm, k_norm)


def _ssd_kernel(*refs, seq, has_h0, emit_state):
    (x_ref, b_ref, c_ref, dt_ref, acs_ref, cwx_ref, cwb_ref, cwc_ref,
     cbx_ref, cbb_ref, cbc_ref, dsk_ref) = refs[:12]
    pos = 12
    h0_ref = None
    if has_h0:
        h0_ref = refs[pos]
        pos += 1
    y_ref = refs[pos]
    pos += 1
    hout_ref = None
    if emit_state:
        hout_ref = refs[pos]
        pos += 1
    pad, xst, btf, cbf, csb, acc, ht, yacc = refs[pos:]
    nc = seq // CHUNK
    half = SSD_R * SSD_P // 2
    halo = 8

    ii = lax.broadcasted_iota(jnp.int32, (CHUNK, CHUNK), 0)
    jj = lax.broadcasted_iota(jnp.int32, (CHUNK, CHUNK), 1)
    lane_lo = jj < SSD_P

    def pair(a, b):
        return jnp.where(lane_lo, a, b)

    pad[0:halo, :] = jnp.zeros((halo, 4 * LANE), F32)
    pad[halo + seq:, :] = jnp.zeros((halo, 4 * LANE), F32)
    pad[halo:halo + seq, 0:2 * LANE] = x_ref[...].astype(F32)
    pad[halo:halo + seq, 2 * LANE:3 * LANE] = b_ref[...].astype(F32)
    pad[halo:halo + seq, 3 * LANE:] = c_ref[...].astype(F32)

    segs = [(cwx_ref, cbx_ref, 0), (cwx_ref, cbx_ref, LANE), (cwb_ref, cbb_ref, 0), (cwc_ref, cbc_ref, 0)]
    for rb in range(nc):
        vals = []
        for blk, (w_ref, bias_ref, lo) in enumerate(segs):
            a = None
            for k in range(CONV_K):
                s = halo + rb * CHUNK + k - CONV_K // 2
                term = pad[s:s + CHUNK, blk * LANE:(blk + 1) * LANE] * w_ref[k:k + 1, lo:lo + LANE]
                a = term if a is None else a + term
            a = a + bias_ref[:, lo:lo + LANE]
            vals.append(a * jax.nn.sigmoid(a))
        for p in range(2):
            v = vals[p]
            yacc[rb * CHUNK:(rb + 1) * CHUNK, p * LANE:(p + 1) * LANE] = v * dsk_ref[:, p * LANE:(p + 1) * LANE]
            vh = v.astype(BF16)
            zh = jnp.zeros_like(vh)
            xst[rb, p, 0:CHUNK, :] = jnp.where(lane_lo, vh, zh)
            xst[rb, p, CHUNK:, :] = jnp.where(lane_lo, zh, vh)
        vb = vals[2]
        vc = vals[3].astype(BF16)
        btf[rb] = vb.T
        csb[rb] = vc
        cbf[rb] = lax.dot_general(vc, vb.astype(BF16), (((1,), (1,)), ((), ())), preferred_element_type=F32)
        acc[rb] = jnp.concatenate([acs_ref[rb]] * (CHUNK // 8), axis=0).T

    for d in range(2):
        if has_h0:
            ht[d, :, 0:half] = h0_ref[d, 0:half, :].T
            ht[d, :, half:] = h0_ref[d, half:, :].T
        else:
            ht[d] = jnp.zeros((SSD_N, 2 * half), F32)

    def chunk(c, d):
        r0 = pl.multiple_of(c * CHUNK, CHUNK)
        cbm = cbf[c]
        bt = btf[c]
        cc = csb[c]
        acol = acc[c]
        arow = acs_ref[c]
        drow = dt_ref[c]
        h_prev = ht[d]
        yoff = jnp.dot(cc, h_prev.astype(BF16), preferred_element_type=F32)
        mask = (ii >= jj) if d == 0 else (ii <= jj)
        cbm = jnp.where(mask, cbm, 0.0)
        last = CHUNK - 1 if d == 0 else 0
        ms, bws, eas, cds = [], [], [], []
        for r in range(SSD_R):
            col = d * SSD_R + r
            acb = jnp.broadcast_to(acol[:, col:col + 1], (CHUNK, CHUNK))
            ar = arow[col:col + 1, :]
            dr = drow[col:col + 1, :]
            dec = jnp.exp2(jnp.minimum(acb - ar, 0.0))
            ms.append((cbm * dec * dr).astype(BF16))
            eas.append(jnp.exp2(acb))
            a_last = ar[:, last:last + 1]
            bws.append((bt * (dr * jnp.exp2(a_last - ar))).astype(BF16))
            cds.append(jnp.exp2(a_last))
        ys, sts, cdm = [], [], []
        for p in range(2):
            r_a, r_b = 2 * p, 2 * p + 1
            x2 = xst[c, p]
            yd = jnp.dot(jnp.concatenate([ms[r_a], ms[r_b]], axis=1), x2, preferred_element_type=F32)
            ys.append(yd + yoff[:, p * LANE:(p + 1) * LANE] * pair(eas[r_a], eas[r_b]))
            sts.append(jnp.dot(jnp.concatenate([bws[r_a], bws[r_b]], axis=1), x2, preferred_element_type=F32))
            cdm.append(pair(cds[r_a], cds[r_b]))
        ht[d] = h_prev * jnp.concatenate(cdm, axis=1) + jnp.concatenate(sts, axis=1)
        yacc[pl.ds(r0, CHUNK), :] += jnp.concatenate(ys, axis=1)

    def body(t, carry):
        chunk(t, 0)
        chunk(nc - 1 - t, 1)
        return carry

    lax.fori_loop(0, nc, body, 0, unroll=True)
    y_ref[...] = yacc[...].astype(BF16)

    if emit_state:
        for d in range(2):
            hout_ref[d, 0:half, :] = ht[d, :, 0:half].T
            hout_ref[d, half:, :] = ht[d, :, half:].T


def _ssd(proj, dt_t, acs_t, conv_w, conv_b, dskip, h0, *, seq, nbatch, row_blk0, emit_state):
    gw = SSD_R * SSD_P
    nc = seq // CHUNK
    has_h0 = h0 is not None
    xb0, bb0, cb0 = COL_X // gw, COL_B // SSD_N, COL_C // SSD_N
    in_specs = [
        pl.BlockSpec((seq, gw), lambda b, g: (row_blk0 + b, xb0 + g)),
        pl.BlockSpec((seq, SSD_N), lambda b, g: (row_blk0 + b, bb0 + g)),
        pl.BlockSpec((seq, SSD_N), lambda b, g: (row_blk0 + b, cb0 + g)),
        pl.BlockSpec((nc, 8, CHUNK), lambda b, g: (row_blk0 + b, g, 0)),
        pl.BlockSpec((nc, 8, CHUNK), lambda b, g: (row_blk0 + b, g, 0)),
        pl.BlockSpec((CONV_K, gw), lambda b, g: (0, g)),
        pl.BlockSpec((CONV_K, SSD_N), lambda b, g: (0, D_INNER // SSD_N + g)),
        pl.BlockSpec((CONV_K, SSD_N), lambda b, g: (0, (D_INNER + BC_DIM) // SSD_N + g)),
        pl.BlockSpec((1, gw), lambda b, g: (0, g)),
        pl.BlockSpec((1, SSD_N), lambda b, g: (0, D_INNER // SSD_N + g)),
        pl.BlockSpec((1, SSD_N), lambda b, g: (0, (D_INNER + BC_DIM) // SSD_N + g)),
        pl.BlockSpec((1, gw), lambda b, g: (0, g)),
    ]
    args = [proj, proj, proj, dt_t, acs_t, conv_w, conv_w, conv_w, conv_b, conv_b, conv_b, dskip]
    state_spec = pl.BlockSpec((None, 2, gw, SSD_N), lambda b, g: (b, 0, g, 0))
    if has_h0:
        in_specs.append(state_spec)
        args.append(h0)
    out_specs = [pl.BlockSpec((seq, gw), lambda b, g: (b, g))]
    out_shape = [jax.ShapeDtypeStruct((nbatch * seq, D_INNER), BF16)]
    if emit_state:
        out_specs.append(state_spec)
        out_shape.append(jax.ShapeDtypeStruct((nbatch, 2, H_SSD * SSD_P, SSD_N), F32))
    return pl.pallas_call(
        functools.partial(_ssd_kernel, seq=seq, has_h0=has_h0, emit_state=emit_state),
        grid=(nbatch, SSD_G),
        in_specs=in_specs,
        out_specs=out_specs,
        out_shape=out_shape,
        scratch_shapes=[pltpu.VMEM((seq + 16, 4 * LANE), F32),
                        pltpu.VMEM((nc, 2, 2 * CHUNK, LANE), BF16),
                        pltpu.VMEM((nc, SSD_N, CHUNK), F32),
                        pltpu.VMEM((nc, CHUNK, CHUNK), F32),
                        pltpu.VMEM((nc, CHUNK, SSD_N), BF16),
                        pltpu.VMEM((nc, CHUNK, LANE), F32),
                        pltpu.VMEM((2, SSD_N, gw), F32),
                        pltpu.VMEM((seq, gw), F32)],
        compiler_params=_params(("arbitrary", "arbitrary")),
        name="ssd_lat" if has_h0 else "ssd_ctx",
    )(*args)


def _merge_kernel(ac_ref, al_ref, yc_ref, yl_ref, z_ref, g_ref, xp_ref, xl_ref, mod_ref, nrm_ref,
                  wa_ref, ws_ref, wo_ref, l1g_ref, l1b_ref, o_ref, *, tm):
    is_ctx = pl.program_id(0) < N_CTX // tm
    attn = jnp.where(is_ctx, ac_ref[...], al_ref[...])
    y = jnp.where(is_ctx, yc_ref[...], yl_ref[...])
    x = jnp.where(is_ctx, xp_ref[...], xl_ref[...])
    z = z_ref[...].astype(F32)
    yg = y.astype(F32) * (z * jax.nn.sigmoid(z))
    yn = yg * lax.rsqrt(jnp.mean(yg * yg, axis=-1, keepdims=True) + EPS) * nrm_ref[...]
    pa = jnp.dot(attn, wa_ref[...], preferred_element_type=F32)
    ps = jnp.dot(yn.astype(BF16), ws_ref[...], preferred_element_type=F32)
    g = g_ref[...].astype(F32)
    merged = jax.nn.sigmoid(g[:, :D_MODEL]) * pa + jax.nn.sigmoid(g[:, D_MODEL:]) * ps
    mix = jnp.dot(merged.astype(BF16), wo_ref[...], preferred_element_type=F32)
    x1 = _ln(ALPHA * x + mod_ref[2:3, :] * mix)
    o_ref[...] = x1 * l1g_ref[...] + l1b_ref[...]


def _merge(attn_c, attn_l, y_c, y_l, proj, xp, xl, mod, ssd_norm, wa, ws, wo, l1g, l1b):
    tm = 256
    const = lambda shape: pl.BlockSpec(shape, lambda i: (0, 0))
    ctx_rows = lambda w: pl.BlockSpec((tm, w), lambda i: (_ctx_blk(i, tm), 0))
    lat_rows = lambda w: pl.BlockSpec((tm, w), lambda i: (_lat_blk(i, tm), 0))
    return pl.pallas_call(
        functools.partial(_merge_kernel, tm=tm),
        grid=(N_TOK // tm,),
        in_specs=[ctx_rows(Q_DIM), lat_rows(Q_DIM), ctx_rows(D_INNER), lat_rows(D_INNER),
                  pl.BlockSpec((tm, D_INNER), lambda i: (i, COL_Z // D_INNER)),
                  pl.BlockSpec((tm, 2 * D_MODEL), lambda i: (i, COL_GATE // (2 * D_MODEL))),
                  pl.BlockSpec((tm, D_MODEL), lambda i: (_ctx_blk(i, tm), 0)),
                  pl.BlockSpec((tm, D_MODEL), lambda i: (_lat_blk(i, tm), 0)),
                  pl.BlockSpec((None, 6, D_MODEL), lambda i: (_mod_row(i, tm), 0, 0)),
                  const((1, D_INNER)), const((Q_DIM, D_MODEL)), const((D_INNER, D_MODEL)),
                  const((D_MODEL, D_MODEL)), const((1, D_MODEL)), const((1, D_MODEL))],
        out_specs=pl.BlockSpec((tm, D_MODEL), lambda i: (i, 0)),
        out_shape=jax.ShapeDtypeStruct((N_TOK, D_MODEL), F32),
        compiler_params=_params(("arbitrary",)),
        name="merge",
    )(attn_c, attn_l, y_c, y_l, proj, proj, xp, xl, mod, ssd_norm, wa, ws, wo, l1g, l1b)


def _mlp_kernel(x1_ref, mod_ref, w1_ref, b1_ref, w2_ref, b2_ref, l2g_ref, l2b_ref, op_ref, ol_ref,
                h_scr, acc_scr, *, tm):
    i = pl.program_id(0)
    j = pl.program_id(1)

    @pl.when(j == 0)
    def _():
        h = _ln(x1_ref[...]) * (1.0 + mod_ref[4:5, :]) + mod_ref[3:4, :]
        h_scr[...] = h.astype(BF16)
        acc_scr[...] = jnp.zeros_like(acc_scr)

    u = jnp.dot(h_scr[...], w1_ref[...].astype(BF16), preferred_element_type=F32) + b1_ref[...]
    u = jnp.maximum(u, 0.0)
    acc_scr[...] += jnp.dot((u * u).astype(BF16), w2_ref[...].astype(BF16), preferred_element_type=F32)

    def result():
        ff = acc_scr[...] + b2_ref[...]
        x2 = _ln(ALPHA * x1_ref[...] + mod_ref[5:6, :] * ff)
        return x2 * l2g_ref[...] + l2b_ref[...]

    last = j == pl.num_programs(1) - 1
    is_ctx = i < N_CTX // tm

    @pl.when(last & is_ctx)
    def _():
        op_ref[...] = result()

    @pl.when(last & jnp.logical_not(is_ctx))
    def _():
        ol_ref[...] = result()


def _mlp(x1, mod, w1, b1, w2, b2, l2g, l2b):
    tm, tf = 1024, 512
    return pl.pallas_call(
        functools.partial(_mlp_kernel, tm=tm),
        grid=(N_TOK // tm, D_FF // tf),
        in_specs=[pl.BlockSpec((tm, D_MODEL), lambda i, j: (i, 0)),
                  pl.BlockSpec((None, 6, D_MODEL), lambda i, j: (_mod_row(i, tm), 0, 0)),
                  pl.BlockSpec((D_MODEL, tf), lambda i, j: (0, j)),
                  pl.BlockSpec((1, tf), lambda i, j: (0, j)),
                  pl.BlockSpec((tf, D_MODEL), lambda i, j: (j, 0)),
                  pl.BlockSpec((1, D_MODEL), lambda i, j: (0, 0)),
                  pl.BlockSpec((1, D_MODEL), lambda i, j: (0, 0)),
                  pl.BlockSpec((1, D_MODEL), lambda i, j: (0, 0))],
        out_specs=[pl.BlockSpec((tm, D_MODEL), lambda i, j: (_ctx_blk(i, tm), 0)),
                   pl.BlockSpec((tm, D_MODEL), lambda i, j: (_lat_blk(i, tm), 0))],
        out_shape=[jax.ShapeDtypeStruct((N_CTX, D_MODEL), F32), jax.ShapeDtypeStruct((N_LAT, D_MODEL), F32)],
        scratch_shapes=[pltpu.VMEM((tm, D_MODEL), BF16), pltpu.VMEM((tm, D_MODEL), F32)],
        compiler_params=_params(("arbitrary", "arbitrary")),
        name="mlp",
    )(x1, mod, w1, b1, w2, b2, l2g, l2b)


def _dt_perm():
    idx = []
    for n in range(DT_DIM):
        g, d, r = n // 8, (n % 8) // SSD_R, n % SSD_R
        idx.append(d * H_SSD + g * SSD_R + r)
    return jnp.asarray(idx, dtype=jnp.int32)


def _rope_tables():
    t = np.arange(DEC_SEQ)
    posn = np.stack([t // GRID_W, t % GRID_W], axis=-1).astype(np.float32)
    inv = (ROPE_THETA ** (-np.arange(ROPE_F, dtype=np.float32) / ROPE_F)).astype(np.float32)
    ang = posn[:, :, None] * inv
    cos, sin = np.cos(ang), np.sin(ang)
    zero = np.zeros_like(sin[:, 0])
    cos_t = np.concatenate([cos[:, 0], cos[:, 0], cos[:, 1], cos[:, 1]], axis=-1)
    sa_t = np.concatenate([-sin[:, 0], zero, -sin[:, 1], zero], axis=-1)
    sb_t = np.concatenate([zero, sin[:, 0], zero, sin[:, 1]], axis=-1)
    return tuple(jnp.asarray(a, dtype=F32) for a in (cos_t, sa_t, sb_t))


def kernel(x_prompt, x_sample, cache_k, cache_v, state_ssd, c, c_ctx, w_mod, b_mod, w_in, q_norm, k_norm, conv_w, conv_b, a_log, dt_bias, d_skip, ssd_norm, w_attn_o, w_ssd_o, w_out, ln1_g, ln1_b, w_mlp1, b_mlp1, w_mlp2, b_mlp2, ln2_g, ln2_b):
    l = 0
    perm = _dt_perm()
    w = w_in[l]
    o_q, o_k, o_v, o_z = 0, Q_DIM, Q_DIM + KV_DIM, Q_DIM + 2 * KV_DIM
    o_xbc = o_z + D_INNER
    o_dt = o_xbc + XBC_DIM
    o_gate = o_dt + DT_DIM
    w_dt = jnp.concatenate([jnp.take(w[:, o_dt:o_dt + DT_DIM], perm, axis=1),
                            jnp.zeros((D_MODEL, LANE - DT_DIM), F32)], axis=1).astype(BF16)
    w_main = jnp.concatenate(
        [w[:, o_z:o_z + D_INNER], w[:, o_gate:o_gate + 2 * D_MODEL], w[:, o_xbc:o_xbc + XBC_DIM],
         w[:, o_q:o_q + Q_DIM], w[:, o_k:o_k + KV_DIM], w[:, o_v:o_v + KV_DIM],
         jnp.zeros((D_MODEL, PROJ_W - COL_DT), F32)], axis=1).astype(BF16)
    pad = jnp.zeros((LANE - DT_DIM,), F32)
    bias_col = jnp.concatenate([jnp.take(dt_bias[l].reshape(-1), perm), pad]).reshape(LANE, 1)
    alog_col = jnp.concatenate([jnp.take(a_log[l].reshape(-1), perm), pad]).reshape(LANE, 1)
    dskip_row = jnp.repeat(d_skip[l], SSD_P).reshape(1, D_INNER)
    row = lambda v: v.reshape(1, -1)

    c_all = jnp.concatenate([c_ctx[None, :], c, jnp.zeros((8 - 1 - DEC_BATCH, D_MODEL), F32)], axis=0)
    mod = _modulation(c_all, w_mod[l], row(b_mod[l])).reshape(8, 6, D_MODEL)

    xp = x_prompt.reshape(N_CTX, D_MODEL)
    xl = x_sample.reshape(N_LAT, D_MODEL)
    proj, dt_raw = _inproj(xp, xl, mod, w_main, w_dt)
    dt_t, acs_t = _dtprep(dt_raw, bias_col, alog_col)

    qn, kn = row(q_norm[l]), row(k_norm[l])
    attn_c, k_new, v_new = _attn_ctx(proj, qn, kn)
    cos_t, sa_t, sb_t = _rope_tables()
    attn_l = _attn_lat(proj, cache_k, cache_v, cos_t, sa_t, sb_t, qn, kn)

    cw, cb = conv_w[l], row(conv_b[l])
    y_c, h_new = _ssd(proj, dt_t, acs_t, cw, cb, dskip_row, None, seq=SEQ, nbatch=BATCH,
                      row_blk0=0, emit_state=True)
    h0 = state_ssd[:, l].reshape(DEC_BATCH, 2, H_SSD * SSD_P, SSD_N)
    (y_l,) = _ssd(proj, dt_t, acs_t, cw, cb, dskip_row, h0, seq=DEC_SEQ, nbatch=DEC_BATCH,
                  row_blk0=N_CTX // DEC_SEQ, emit_state=False)

    x1 = _merge(attn_c, attn_l, y_c, y_l, proj, xp, xl, mod, row(ssd_norm[l]), w_attn_o[l].astype(BF16),
                w_ssd_o[l].astype(BF16), w_out[l].astype(BF16), row(ln1_g[l]), row(ln1_b[l]))
    out_p, out_l = _mlp(x1, mod, w_mlp1[l], row(b_mlp1[l]), w_mlp2[l], row(b_mlp2[l]),
                        row(ln2_g[l]), row(ln2_b[l]))

    y_p = out_p.reshape(BATCH, SEQ, D_MODEL)
    y_s = out_l.reshape(DEC_BATCH, DEC_SEQ, D_MODEL)
    new_state = h_new.reshape(BATCH, DEPTH, 2, H_SSD, SSD_P, SSD_N)
    return (y_p, y_s, k_new, v_new, new_state)
```

```python
import functools

import jax
import jax.numpy as jnp
import numpy as np
from jax import lax
from jax.experimental import pallas as pl
from jax.experimental.pallas import tpu as pltpu

D_MODEL = 1024
BATCH = 16
SEQ = 256
DEC_BATCH = 4
DEC_SEQ = 1024
PAST_LEN = 512
GRID_W = 64
H_Q = 8
H_KV = 2
HD = 128
ROPE_F = HD // 4
ROPE_THETA = 10000.0
D_INNER = 2 * D_MODEL
SSD_P = 64
H_SSD = D_INNER // SSD_P
SSD_G = 8
SSD_R = H_SSD // SSD_G
SSD_N = 128
CONV_K = 5
CHUNK = 128
D_FF = 4 * D_MODEL
DEPTH = 1
ALPHA = (2.0 * DEPTH) ** 0.25
EPS = 1e-6
LOG2E = 1.4426950408889634
Q_DIM = H_Q * HD
KV_DIM = H_KV * HD
BC_DIM = SSD_G * SSD_N
XBC_DIM = D_INNER + 2 * BC_DIM
DT_DIM = 2 * H_SSD
REP = H_Q // H_KV

N_CTX = BATCH * SEQ
N_LAT = DEC_BATCH * DEC_SEQ
N_TOK = N_CTX + N_LAT

LANE = 128

COL_Z = 0
COL_GATE = COL_Z + D_INNER
COL_X = COL_GATE + 2 * D_MODEL
COL_B = COL_X + D_INNER
COL_C = COL_B + BC_DIM
COL_Q = COL_C + BC_DIM
COL_K = COL_Q + Q_DIM
COL_V = COL_K + KV_DIM
COL_DT = COL_V + KV_DIM
INPROJ_TN = 2560
PROJ_W = -(-COL_DT // INPROJ_TN) * INPROJ_TN

VMEM_LIMIT = 52 * 1024 * 1024

F32 = jnp.float32
BF16 = jnp.bfloat16


def _ln(x):
    mu = jnp.mean(x, axis=-1, keepdims=True)
    xc = x - mu
    var = jnp.mean(xc * xc, axis=-1, keepdims=True)
    return xc * lax.rsqrt(var + EPS)


def _rms(x, g):
    return x * lax.rsqrt(jnp.mean(x * x, axis=-1, keepdims=True) + EPS) * g


def _mod_row(i, tm):
    nct = N_CTX // tm
    tpl = DEC_SEQ // tm
    return jnp.where(i < nct, 0, 1 + jnp.maximum(i - nct, 0) // tpl)


def _ctx_blk(i, tm):
    return jnp.minimum(i, N_CTX // tm - 1)


def _lat_blk(i, tm):
    return jnp.maximum(i - N_CTX // tm, 0)


def _params(sem):
    return pltpu.CompilerParams(dimension_semantics=sem, vmem_limit_bytes=VMEM_LIMIT)


def _mod_kernel(c_ref, w_ref, b_ref, o_ref):
    c = c_ref[...]
    s = c * jax.nn.sigmoid(c)
    o_ref[...] = jnp.dot(s.astype(BF16), w_ref[...].astype(BF16), preferred_element_type=F32) + b_ref[...]


def _modulation(c_all, w_mod, b_mod):
    tn = 1536
    return pl.pallas_call(
        _mod_kernel,
        grid=(6 * D_MODEL // tn,),
        in_specs=[pl.BlockSpec((8, D_MODEL), lambda j: (0, 0)),
                  pl.BlockSpec((D_MODEL, tn), lambda j: (0, j)),
                  pl.BlockSpec((1, tn), lambda j: (0, j))],
        out_specs=pl.BlockSpec((8, tn), lambda j: (0, j)),
        out_shape=jax.ShapeDtypeStruct((8, 6 * D_MODEL), F32),
        compiler_params=_params(("arbitrary",)),
        name="modulation",
    )(c_all, w_mod, b_mod)


def _inproj_kernel(xp_ref, xl_ref, mod_ref, w_ref, wdt_ref, o_ref, dt_ref, h_scr, *, tm):
    @pl.when(pl.program_id(1) == 0)
    def _():
        x = jnp.where(pl.program_id(0) < N_CTX // tm, xp_ref[...], xl_ref[...])
        h = _ln(x) * (1.0 + mod_ref[1:2, :]) + mod_ref[0:1, :]
        h_scr[...] = h.astype(BF16)
        dt_ref[...] = jnp.dot(h_scr[...], wdt_ref[...], preferred_element_type=F32)

    o_ref[...] = jnp.dot(h_scr[...], w_ref[...], preferred_element_type=F32).astype(BF16)


def _inproj(xp, xl, mod, w_main, w_dt):
    tm, tn = 1024, INPROJ_TN
    return pl.pallas_call(
        functools.partial(_inproj_kernel, tm=tm),
        grid=(N_TOK // tm, PROJ_W // tn),
        in_specs=[pl.BlockSpec((tm, D_MODEL), lambda i, j: (_ctx_blk(i, tm), 0)),
                  pl.BlockSpec((tm, D_MODEL), lambda i, j: (_lat_blk(i, tm), 0)),
                  pl.BlockSpec((None, 6, D_MODEL), lambda i, j: (_mod_row(i, tm), 0, 0)),
                  pl.BlockSpec((D_MODEL, tn), lambda i, j: (0, j)),
                  pl.BlockSpec((D_MODEL, LANE), lambda i, j: (0, 0))],
        out_specs=[pl.BlockSpec((tm, tn), lambda i, j: (i, j)),
                   pl.BlockSpec((tm, LANE), lambda i, j: (i, 0))],
        out_shape=[jax.ShapeDtypeStruct((N_TOK, PROJ_W), BF16),
                   jax.ShapeDtypeStruct((N_TOK, LANE), F32)],
        scratch_shapes=[pltpu.VMEM((tm, D_MODEL), BF16)],
        compiler_params=_params(("arbitrary", "arbitrary")),
        name="inproj",
    )(xp, xl, mod, w_main, w_dt)


def _dtprep_kernel(p_ref, bias_ref, alog_ref, dt_ref, acs_ref, *, nchunk):
    a_head = -jnp.exp(alog_ref[...]) * LOG2E
    ii = lax.broadcasted_iota(jnp.int32, (CHUNK, CHUNK), 0)
    jj = lax.broadcasted_iota(jnp.int32, (CHUNK, CHUNK), 1)
    upper = (ii <= jj).astype(F32)
    lower = (ii >= jj).astype(F32)
    is_fwd = (ii & 7) < SSD_R
    for c in range(nchunk):
        t = p_ref[c * CHUNK:(c + 1) * CHUNK, :]
        tt = t.T + bias_ref[...]
        dtv = jnp.maximum(tt, 0.0) + jnp.log1p(jnp.exp(-jnp.abs(tt)))
        a = dtv * a_head
        pre = jnp.dot(a, upper, precision=lax.Precision.HIGHEST, preferred_element_type=F32)
        suf = jnp.dot(a, lower, precision=lax.Precision.HIGHEST, preferred_element_type=F32)
        dt_ref[c] = dtv
        acs_ref[c] = jnp.where(is_fwd, pre, suf)


def _dtprep(proj, bias_col, alog_col):
    nchunk = 4
    tm = nchunk * CHUNK
    nct = N_TOK // CHUNK
    out = jax.ShapeDtypeStruct((nct, LANE, CHUNK), F32)
    return pl.pallas_call(
        functools.partial(_dtprep_kernel, nchunk=nchunk),
        grid=(N_TOK // tm,),
        in_specs=[pl.BlockSpec((tm, LANE), lambda i: (i, 0)),
                  pl.BlockSpec((LANE, 1), lambda i: (0, 0)),
                  pl.BlockSpec((LANE, 1), lambda i: (0, 0))],
        out_specs=[pl.BlockSpec((nchunk, LANE, CHUNK), lambda i: (i, 0, 0)),
                   pl.BlockSpec((nchunk, LANE, CHUNK), lambda i: (i, 0, 0))],
        out_shape=[out, out],
        compiler_params=_params(("arbitrary",)),
        name="dtprep",
    )(proj, bias_col, alog_col)


def _rope(x, cos, sa, sb):
    return x * cos + pltpu.roll(x, HD - ROPE_F, axis=1) * sa + pltpu.roll(x, ROPE_F, axis=1) * sb


def _softmax_pv(q4, k_b, v_b):
    s = lax.dot_general(q4.astype(BF16), k_b, (((1,), (1,)), ((), ())), preferred_element_type=F32)
    m = jnp.max(s, axis=-1, keepdims=True)
    p = jnp.exp2(s - m)
    l = jnp.sum(p, axis=-1, keepdims=True)
    o = jnp.dot(p.astype(BF16), v_b, preferred_element_type=F32)
    return o / l


def _attn_ctx_kernel(q_ref, k_ref, v_ref, qn_ref, kn_ref, o_ref, ko_ref, vo_ref):
    qg = qn_ref[...] * (LOG2E * HD ** -0.5)
    for g in range(H_KV):
        k = _rms(k_ref[:, g * HD:(g + 1) * HD].astype(F32), kn_ref[...])
        v = v_ref[:, g * HD:(g + 1) * HD]
        ko_ref[:, g, :] = k
        vo_ref[:, g, :] = v.astype(F32)
        q4 = jnp.concatenate(
            [_rms(q_ref[:, (g * REP + h) * HD:(g * REP + h + 1) * HD].astype(F32), qg) for h in range(REP)], axis=0)
        o = _softmax_pv(q4, k.astype(BF16), v)
        for h in range(REP):
            o_ref[:, (g * REP + h) * HD:(g * REP + h + 1) * HD] = o[h * SEQ:(h + 1) * SEQ].astype(BF16)


def _attn_ctx(proj, q_norm, k_norm):
    kv_out = jax.ShapeDtypeStruct((BATCH, DEPTH, SEQ, H_KV, HD), F32)
    kv_spec = pl.BlockSpec((None, None, SEQ, H_KV, HD), lambda b: (b, 0, 0, 0, 0))
    return pl.pallas_call(
        _attn_ctx_kernel,
        grid=(BATCH,),
        in_specs=[pl.BlockSpec((SEQ, Q_DIM), lambda b: (b, COL_Q // Q_DIM)),
                  pl.BlockSpec((SEQ, KV_DIM), lambda b: (b, COL_K // KV_DIM)),
                  pl.BlockSpec((SEQ, KV_DIM), lambda b: (b, COL_V // KV_DIM)),
                  pl.BlockSpec((1, HD), lambda b: (0, 0)),
                  pl.BlockSpec((1, HD), lambda b: (0, 0))],
        out_specs=[pl.BlockSpec((SEQ, Q_DIM), lambda b: (b, 0)), kv_spec, kv_spec],
        out_shape=[jax.ShapeDtypeStruct((N_CTX, Q_DIM), BF16), kv_out, kv_out],
        compiler_params=_params(("arbitrary",)),
        name="attn_ctx",
    )(proj, proj, proj, q_norm, k_norm)


def _attn_lat_kernel(q_ref, k_ref, v_ref, ck_ref, cv_ref, cos_ref, sa_ref, sb_ref, qn_ref, kn_ref,
                     o_ref, kall, vall, *, tq):
    qi = pl.program_id(1)

    @pl.when(qi == 0)
    def _():
        for g in range(H_KV):
            k = _rms(k_ref[:, g * HD:(g + 1) * HD].astype(F32), kn_ref[...])
            k = _rope(k, cos_ref[...], sa_ref[...], sb_ref[...])
            kall[g, 0:PAST_LEN, :] = ck_ref[:, g, :].astype(BF16)
            kall[g, PAST_LEN:, :] = k.astype(BF16)
            vall[g, 0:PAST_LEN, :] = cv_ref[:, g, :].astype(BF16)
            vall[g, PAST_LEN:, :] = v_ref[:, g * HD:(g + 1) * HD]

    r0 = pl.multiple_of(qi * tq, tq)
    cos = cos_ref[pl.ds(r0, tq), :]
    sa = sa_ref[pl.ds(r0, tq), :]
    sb = sb_ref[pl.ds(r0, tq), :]
    qg = qn_ref[...] * (LOG2E * HD ** -0.5)
    for h in range(H_Q):
        g = h // REP
        qh = _rope(_rms(q_ref[:, h * HD:(h + 1) * HD].astype(F32), qg), cos, sa, sb)
        o_ref[:, h * HD:(h + 1) * HD] = _softmax_pv(qh, kall[g], vall[g]).astype(BF16)


def _attn_lat(proj, cache_k, cache_v, cos, sa, sb, q_norm, k_norm):
    tq = 256
    nq = DEC_SEQ // tq
    row0 = N_CTX // tq
    tab = pl.BlockSpec((DEC_SEQ, HD), lambda b, i: (0, 0))
    cache_spec = pl.BlockSpec((None, None, PAST_LEN, H_KV, HD), lambda b, i: (b, 0, 0, 0, 0))
    return pl.pallas_call(
        functools.partial(_attn_lat_kernel, tq=tq),
        grid=(DEC_BATCH, nq),
        in_specs=[pl.BlockSpec((tq, Q_DIM), lambda b, i: (row0 + b * nq + i, COL_Q // Q_DIM)),
                  pl.BlockSpec((DEC_SEQ, KV_DIM), lambda b, i: (N_CTX // DEC_SEQ + b, COL_K // KV_DIM)),
                  pl.BlockSpec((DEC_SEQ, KV_DIM), lambda b, i: (N_CTX // DEC_SEQ + b, COL_V // KV_DIM)),
                  cache_spec, cache_spec,
                  tab, tab, tab,
                  pl.BlockSpec((1, HD), lambda b, i: (0, 0)),
                  pl.BlockSpec((1, HD), lambda b, i: (0, 0))],
        out_specs=pl.BlockSpec((tq, Q_DIM), lambda b, i: (b * nq + i, 0)),
        out_shape=jax.ShapeDtypeStruct((N_LAT, Q_DIM), BF16),
        scratch_shapes=[pltpu.VMEM((H_KV, PAST_LEN + DEC_SEQ, HD), BF16),
                        pltpu.VMEM((H_KV, PAST_LEN + DEC_SEQ, HD), BF16)],
        compiler_params=_params(("arbitrary", "arbitrary")),
        name="attn_lat",
    )(proj, proj, proj, cache_k, cache_v, cos, sa, sb, q_norm, k_norm)


def _ssd_kernel(*refs, seq, has_h0, emit_state):
    (x_ref, b_ref, c_ref, dt_ref, acs_ref, cwx_ref, cwb_ref, cwc_ref,
     cbx_ref, cbb_ref, cbc_ref, dsk_ref) = refs[:12]
    pos = 12
    h0_ref = None
    if has_h0:
        h0_ref = refs[pos]
        pos += 1
    y_ref = refs[pos]
    pos += 1
    hout_ref = None
    if emit_state:
        hout_ref = refs[pos]
        pos += 1
    pad, xst, btf, cbf, csb, acc, ht, yacc = refs[pos:]
    nc = seq // CHUNK
    half = SSD_R * SSD_P // 2
    halo = 8

    ii = lax.broadcasted_iota(jnp.int32, (CHUNK, CHUNK), 0)
    jj = lax.broadcasted_iota(jnp.int32, (CHUNK, CHUNK), 1)
    lane_lo = jj < SSD_P

    def pair(a, b):
        return jnp.where(lane_lo, a, b)

    pad[0:halo, :] = jnp.zeros((halo, 4 * LANE), F32)
    pad[halo + seq:, :] = jnp.zeros((halo, 4 * LANE), F32)
    pad[halo:halo + seq, 0:2 * LANE] = x_ref[...].astype(F32)
    pad[halo:halo + seq, 2 * LANE:3 * LANE] = b_ref[...].astype(F32)
    pad[halo:halo + seq, 3 * LANE:] = c_ref[...].astype(F32)

    segs = [(cwx_ref, cbx_ref, 0), (cwx_ref, cbx_ref, LANE), (cwb_ref, cbb_ref, 0), (cwc_ref, cbc_ref, 0)]
    for rb in range(nc):
        vals = []
        for blk, (w_ref, bias_ref, lo) in enumerate(segs):
            a = None
            for k in range(CONV_K):
                s = halo + rb * CHUNK + k - CONV_K // 2
                term = pad[s:s + CHUNK, blk * LANE:(blk + 1) * LANE] * w_ref[k:k + 1, lo:lo + LANE]
                a = term if a is None else a + term
            a = a + bias_ref[:, lo:lo + LANE]
            vals.append(a * jax.nn.sigmoid(a))
        for p in range(2):
            v = vals[p]
            yacc[rb * CHUNK:(rb + 1) * CHUNK, p * LANE:(p + 1) * LANE] = v * dsk_ref[:, p * LANE:(p + 1) * LANE]
            vh = v.astype(BF16)
            zh = jnp.zeros_like(vh)
            xst[rb, p, 0:CHUNK, :] = jnp.where(lane_lo, vh, zh)
            xst[rb, p, CHUNK:, :] = jnp.where(lane_lo, zh, vh)
        vb = vals[2]
        vc = vals[3].astype(BF16)
        btf[rb] = vb.T
        csb[rb] = vc
        cbf[rb] = lax.dot_general(vc, vb.astype(BF16), (((1,), (1,)), ((), ())), preferred_element_type=F32)
        acc[rb] = jnp.concatenate([acs_ref[rb]] * (CHUNK // 8), axis=0).T

    for d in range(2):
        if has_h0:
            ht[d, :, 0:half] = h0_ref[d, 0:half, :].T
            ht[d, :, half:] = h0_ref[d, half:, :].T
        else:
            ht[d] = jnp.zeros((SSD_N, 2 * half), F32)

    def chunk(c, d):
        r0 = pl.multiple_of(c * CHUNK, CHUNK)
        cbm = cbf[c]
        bt = btf[c]
        cc = csb[c]
        acol = acc[c]
        arow = acs_ref[c]
        drow = dt_ref[c]
        h_prev = ht[d]
        yoff = jnp.dot(cc, h_prev.astype(BF16), preferred_element_type=F32)
        mask = (ii >= jj) if d == 0 else (ii <= jj)
        cbm = jnp.where(mask, cbm, 0.0)
        last = CHUNK - 1 if d == 0 else 0
        ms, bws, eas, cds = [], [], [], []
        for r in range(SSD_R):
            col = d * SSD_R + r
            acb = jnp.broadcast_to(acol[:, col:col + 1], (CHUNK, CHUNK))
            ar = arow[col:col + 1, :]
            dr = drow[col:col + 1, :]
            dec = jnp.exp2(jnp.minimum(acb - ar, 0.0))
            ms.append((cbm * dec * dr).astype(BF16))
            eas.append(jnp.exp2(acb))
            a_last = ar[:, last:last + 1]
            bws.append((bt * (dr * jnp.exp2(a_last - ar))).astype(BF16))
            cds.append(jnp.exp2(a_last))
        ys, sts, cdm = [], [], []
        for p in range(2):
            r_a, r_b = 2 * p, 2 * p + 1
            x2 = xst[c, p]
            yd = jnp.dot(jnp.concatenate([ms[r_a], ms[r_b]], axis=1), x2, preferred_element_type=F32)
            ys.append(yd + yoff[:, p * LANE:(p + 1) * LANE] * pair(eas[r_a], eas[r_b]))
            sts.append(jnp.dot(jnp.concatenate([bws[r_a], bws[r_b]], axis=1), x2, preferred_element_type=F32))
            cdm.append(pair(cds[r_a], cds[r_b]))
        ht[d] = h_prev * jnp.concatenate(cdm, axis=1) + jnp.concatenate(sts, axis=1)
        yacc[pl.ds(r0, CHUNK), :] += jnp.concatenate(ys, axis=1)

    def body(t, carry):
        chunk(t, 0)
        chunk(nc - 1 - t, 1)
        return carry

    lax.fori_loop(0, nc, body, 0, unroll=True)
    y_ref[...] = yacc[...].astype(BF16)

    if emit_state:
        for d in range(2):
            hout_ref[d, 0:half, :] = ht[d, :, 0:half].T
            hout_ref[d, half:, :] = ht[d, :, half:].T


def _ssd_multi_kernel(*refs, gps, seq, has_h0, emit_state):
    gw = SSD_R * SSD_P
    n_io = 12 + int(has_h0) + 1 + int(emit_state)
    io, scratch = refs[:n_io], refs[n_io:]
    widths = [gw, SSD_N, SSD_N, None, None, gw, SSD_N, SSD_N, gw, SSD_N, SSD_N, gw]
    for gi in range(gps):
        views = []
        for n, r in enumerate(io):
            if n in (3, 4):
                views.append(r.at[:, gi * 8:(gi + 1) * 8, :])
            elif n < 12:
                views.append(r.at[:, gi * widths[n]:(gi + 1) * widths[n]])
            elif n == 12 + int(has_h0):
                views.append(r.at[:, gi * gw:(gi + 1) * gw])
            else:
                views.append(r.at[:, gi * gw:(gi + 1) * gw, :])
        _ssd_kernel(*views, *[s.at[gi] for s in scratch], seq=seq, has_h0=has_h0, emit_state=emit_state)


def _ssd(proj, dt_t, acs_t, conv_w, conv_b, dskip, h0, *, seq, nbatch, row_blk0, emit_state, gps):
    gw = SSD_R * SSD_P * gps
    nw = SSD_N * gps
    nc = seq // CHUNK
    has_h0 = h0 is not None
    xb0, bb0, cb0 = COL_X // gw, COL_B // nw, COL_C // nw
    in_specs = [
        pl.BlockSpec((seq, gw), lambda b, g: (row_blk0 + b, xb0 + g)),
        pl.BlockSpec((seq, nw), lambda b, g: (row_blk0 + b, bb0 + g)),
        pl.BlockSpec((seq, nw), lambda b, g: (row_blk0 + b, cb0 + g)),
        pl.BlockSpec((nc, 8 * gps, CHUNK), lambda b, g: (row_blk0 + b, g, 0)),
        pl.BlockSpec((nc, 8 * gps, CHUNK), lambda b, g: (row_blk0 + b, g, 0)),
        pl.BlockSpec((CONV_K, gw), lambda b, g: (0, g)),
        pl.BlockSpec((CONV_K, nw), lambda b, g: (0, D_INNER // nw + g)),
        pl.BlockSpec((CONV_K, nw), lambda b, g: (0, (D_INNER + BC_DIM) // nw + g)),
        pl.BlockSpec((1, gw), lambda b, g: (0, g)),
        pl.BlockSpec((1, nw), lambda b, g: (0, D_INNER // nw + g)),
        pl.BlockSpec((1, nw), lambda b, g: (0, (D_INNER + BC_DIM) // nw + g)),
        pl.BlockSpec((1, gw), lambda b, g: (0, g)),
    ]
    args = [proj, proj, proj, dt_t, acs_t, conv_w, conv_w, conv_w, conv_b, conv_b, conv_b, dskip]
    state_spec = pl.BlockSpec((None, 2, gw, SSD_N), lambda b, g: (b, 0, g, 0))
    if has_h0:
        in_specs.append(state_spec)
        args.append(h0)
    out_specs = [pl.BlockSpec((seq, gw), lambda b, g: (b, g))]
    out_shape = [jax.ShapeDtypeStruct((nbatch * seq, D_INNER), BF16)]
    if emit_state:
        out_specs.append(state_spec)
        out_shape.append(jax.ShapeDtypeStruct((nbatch, 2, H_SSD * SSD_P, SSD_N), F32))
    g1 = SSD_R * SSD_P
    return pl.pallas_call(
        functools.partial(_ssd_multi_kernel, gps=gps, seq=seq, has_h0=has_h0, emit_state=emit_state),
        grid=(nbatch, SSD_G // gps),
        in_specs=in_specs,
        out_specs=out_specs,
        out_shape=out_shape,
        scratch_shapes=[pltpu.VMEM((gps, seq + 16, 4 * LANE), F32),
                        pltpu.VMEM((gps, nc, 2, 2 * CHUNK, LANE), BF16),
                        pltpu.VMEM((gps, nc, SSD_N, CHUNK), F32),
                        pltpu.VMEM((gps, nc, CHUNK, CHUNK), F32),
                        pltpu.VMEM((gps, nc, CHUNK, SSD_N), BF16),
                        pltpu.VMEM((gps, nc, CHUNK, LANE), F32),
                        pltpu.VMEM((gps, 2, SSD_N, g1), F32),
                        pltpu.VMEM((gps, seq, g1), F32)],
        compiler_params=_params(("arbitrary", "arbitrary")),
        name="ssd_lat" if has_h0 else "ssd_ctx",
    )(*args)


def _merge_kernel(ac_ref, al_ref, yc_ref, yl_ref, z_ref, g_ref, xp_ref, xl_ref, mod_ref, nrm_ref,
                  wa_ref, ws_ref, wo_ref, l1g_ref, l1b_ref, o_ref, *, tm):
    is_ctx = pl.program_id(0) < N_CTX // tm
    attn = jnp.where(is_ctx, ac_ref[...], al_ref[...])
    y = jnp.where(is_ctx, yc_ref[...], yl_ref[...])
    x = jnp.where(is_ctx, xp_ref[...], xl_ref[...])
    z = z_ref[...].astype(F32)
    yg = y.astype(F32) * (z * jax.nn.sigmoid(z))
    yn = yg * lax.rsqrt(jnp.mean(yg * yg, axis=-1, keepdims=True) + EPS) * nrm_ref[...]
    pa = jnp.dot(attn, wa_ref[...], preferred_element_type=F32)
    ps = jnp.dot(yn.astype(BF16), ws_ref[...], preferred_element_type=F32)
    g = g_ref[...].astype(F32)
    merged = jax.nn.sigmoid(g[:, :D_MODEL]) * pa + jax.nn.sigmoid(g[:, D_MODEL:]) * ps
    mix = jnp.dot(merged.astype(BF16), wo_ref[...], preferred_element_type=F32)
    x1 = _ln(ALPHA * x + mod_ref[2:3, :] * mix)
    o_ref[...] = x1 * l1g_ref[...] + l1b_ref[...]


def _merge(attn_c, attn_l, y_c, y_l, proj, xp, xl, mod, ssd_norm, wa, ws, wo, l1g, l1b):
    tm = 256
    const = lambda shape: pl.BlockSpec(shape, lambda i: (0, 0))
    ctx_rows = lambda w: pl.BlockSpec((tm, w), lambda i: (_ctx_blk(i, tm), 0))
    lat_rows = lambda w: pl.BlockSpec((tm, w), lambda i: (_lat_blk(i, tm), 0))
    return pl.pallas_call(
        functools.partial(_merge_kernel, tm=tm),
        grid=(N_TOK // tm,),
        in_specs=[ctx_rows(Q_DIM), lat_rows(Q_DIM), ctx_rows(D_INNER), lat_rows(D_INNER),
                  pl.BlockSpec((tm, D_INNER), lambda i: (i, COL_Z // D_INNER)),
                  pl.BlockSpec((tm, 2 * D_MODEL), lambda i: (i, COL_GATE // (2 * D_MODEL))),
                  pl.BlockSpec((tm, D_MODEL), lambda i: (_ctx_blk(i, tm), 0)),
                  pl.BlockSpec((tm, D_MODEL), lambda i: (_lat_blk(i, tm), 0)),
                  pl.BlockSpec((None, 6, D_MODEL), lambda i: (_mod_row(i, tm), 0, 0)),
                  const((1, D_INNER)), const((Q_DIM, D_MODEL)), const((D_INNER, D_MODEL)),
                  const((D_MODEL, D_MODEL)), const((1, D_MODEL)), const((1, D_MODEL))],
        out_specs=pl.BlockSpec((tm, D_MODEL), lambda i: (i, 0)),
        out_shape=jax.ShapeDtypeStruct((N_TOK, D_MODEL), F32),
        compiler_params=_params(("arbitrary",)),
        name="merge",
    )(attn_c, attn_l, y_c, y_l, proj, proj, xp, xl, mod, ssd_norm, wa, ws, wo, l1g, l1b)


def _mlp_kernel(x1_ref, mod_ref, w1_ref, b1_ref, w2_ref, b2_ref, l2g_ref, l2b_ref, op_ref, ol_ref,
                h_scr, acc_scr, *, tm):
    i = pl.program_id(0)
    j = pl.program_id(1)

    @pl.when(j == 0)
    def _():
        h = _ln(x1_ref[...]) * (1.0 + mod_ref[4:5, :]) + mod_ref[3:4, :]
        h_scr[...] = h.astype(BF16)
        acc_scr[...] = jnp.zeros_like(acc_scr)

    u = jnp.dot(h_scr[...], w1_ref[...].astype(BF16), preferred_element_type=F32) + b1_ref[...]
    u = jnp.maximum(u, 0.0)
    acc_scr[...] += jnp.dot((u * u).astype(BF16), w2_ref[...].astype(BF16), preferred_element_type=F32)

    def result():
        ff = acc_scr[...] + b2_ref[...]
        x2 = _ln(ALPHA * x1_ref[...] + mod_ref[5:6, :] * ff)
        return x2 * l2g_ref[...] + l2b_ref[...]

    last = j == pl.num_programs(1) - 1
    is_ctx = i < N_CTX // tm

    @pl.when(last & is_ctx)
    def _():
        op_ref[...] = result()

    @pl.when(last & jnp.logical_not(is_ctx))
    def _():
        ol_ref[...] = result()


def _mlp(x1, mod, w1, b1, w2, b2, l2g, l2b):
    tm, tf = 1024, 512
    return pl.pallas_call(
        functools.partial(_mlp_kernel, tm=tm),
        grid=(N_TOK // tm, D_FF // tf),
        in_specs=[pl.BlockSpec((tm, D_MODEL), lambda i, j: (i, 0)),
                  pl.BlockSpec((None, 6, D_MODEL), lambda i, j: (_mod_row(i, tm), 0, 0)),
                  pl.BlockSpec((D_MODEL, tf), lambda i, j: (0, j)),
                  pl.BlockSpec((1, tf), lambda i, j: (0, j)),
                  pl.BlockSpec((tf, D_MODEL), lambda i, j: (j, 0)),
                  pl.BlockSpec((1, D_MODEL), lambda i, j: (0, 0)),
                  pl.BlockSpec((1, D_MODEL), lambda i, j: (0, 0)),
                  pl.BlockSpec((1, D_MODEL), lambda i, j: (0, 0))],
        out_specs=[pl.BlockSpec((tm, D_MODEL), lambda i, j: (_ctx_blk(i, tm), 0)),
                   pl.BlockSpec((tm, D_MODEL), lambda i, j: (_lat_blk(i, tm), 0))],
        out_shape=[jax.ShapeDtypeStruct((N_CTX, D_MODEL), F32), jax.ShapeDtypeStruct((N_LAT, D_MODEL), F32)],
        scratch_shapes=[pltpu.VMEM((tm, D_MODEL), BF16), pltpu.VMEM((tm, D_MODEL), F32)],
        compiler_params=_params(("arbitrary", "arbitrary")),
        name="mlp",
    )(x1, mod, w1, b1, w2, b2, l2g, l2b)


def _dt_perm():
    idx = []
    for n in range(DT_DIM):
        g, d, r = n // 8, (n % 8) // SSD_R, n % SSD_R
        idx.append(d * H_SSD + g * SSD_R + r)
    return jnp.asarray(idx, dtype=jnp.int32)


def _rope_tables():
    t = np.arange(DEC_SEQ)
    posn = np.stack([t // GRID_W, t % GRID_W], axis=-1).astype(np.float32)
    inv = (ROPE_THETA ** (-np.arange(ROPE_F, dtype=np.float32) / ROPE_F)).astype(np.float32)
    ang = posn[:, :, None] * inv
    cos, sin = np.cos(ang), np.sin(ang)
    zero = np.zeros_like(sin[:, 0])
    cos_t = np.concatenate([cos[:, 0], cos[:, 0], cos[:, 1], cos[:, 1]], axis=-1)
    sa_t = np.concatenate([-sin[:, 0], zero, -sin[:, 1], zero], axis=-1)
    sb_t = np.concatenate([zero, sin[:, 0], zero, sin[:, 1]], axis=-1)
    return tuple(jnp.asarray(a, dtype=F32) for a in (cos_t, sa_t, sb_t))


def kernel(x_prompt, x_sample, cache_k, cache_v, state_ssd, c, c_ctx, w_mod, b_mod, w_in, q_norm, k_norm, conv_w, conv_b, a_log, dt_bias, d_skip, ssd_norm, w_attn_o, w_ssd_o, w_out, ln1_g, ln1_b, w_mlp1, b_mlp1, w_mlp2, b_mlp2, ln2_g, ln2_b):
    l = 0
    perm = _dt_perm()
    w = w_in[l]
    o_q, o_k, o_v, o_z = 0, Q_DIM, Q_DIM + KV_DIM, Q_DIM + 2 * KV_DIM
    o_xbc = o_z + D_INNER
    o_dt = o_xbc + XBC_DIM
    o_gate = o_dt + DT_DIM
    w_dt = jnp.concatenate([jnp.take(w[:, o_dt:o_dt + DT_DIM], perm, axis=1),
                            jnp.zeros((D_MODEL, LANE - DT_DIM), F32)], axis=1).astype(BF16)
    w_main = jnp.concatenate(
        [w[:, o_z:o_z + D_INNER], w[:, o_gate:o_gate + 2 * D_MODEL], w[:, o_xbc:o_xbc + XBC_DIM],
         w[:, o_q:o_q + Q_DIM], w[:, o_k:o_k + KV_DIM], w[:, o_v:o_v + KV_DIM],
         jnp.zeros((D_MODEL, PROJ_W - COL_DT), F32)], axis=1).astype(BF16)
    pad = jnp.zeros((LANE - DT_DIM,), F32)
    bias_col = jnp.concatenate([jnp.take(dt_bias[l].reshape(-1), perm), pad]).reshape(LANE, 1)
    alog_col = jnp.concatenate([jnp.take(a_log[l].reshape(-1), perm), pad]).reshape(LANE, 1)
    dskip_row = jnp.repeat(d_skip[l], SSD_P).reshape(1, D_INNER)
    row = lambda v: v.reshape(1, -1)

    c_all = jnp.concatenate([c_ctx[None, :], c, jnp.zeros((8 - 1 - DEC_BATCH, D_MODEL), F32)], axis=0)
    mod = _modulation(c_all, w_mod[l], row(b_mod[l])).reshape(8, 6, D_MODEL)

    xp = x_prompt.reshape(N_CTX, D_MODEL)
    xl = x_sample.reshape(N_LAT, D_MODEL)
    proj, dt_raw = _inproj(xp, xl, mod, w_main, w_dt)
    dt_t, acs_t = _dtprep(dt_raw, bias_col, alog_col)

    qn, kn = row(q_norm[l]), row(k_norm[l])
    attn_c, k_new, v_new = _attn_ctx(proj, qn, kn)
    cos_t, sa_t, sb_t = _rope_tables()
    attn_l = _attn_lat(proj, cache_k, cache_v, cos_t, sa_t, sb_t, qn, kn)

    cw, cb = conv_w[l], row(conv_b[l])
    y_c, h_new = _ssd(proj, dt_t, acs_t, cw, cb, dskip_row, None, seq=SEQ, nbatch=BATCH,
                      row_blk0=0, emit_state=True, gps=4)
    h0 = state_ssd[:, l].reshape(DEC_BATCH, 2, H_SSD * SSD_P, SSD_N)
    (y_l,) = _ssd(proj, dt_t, acs_t, cw, cb, dskip_row, h0, seq=DEC_SEQ, nbatch=DEC_BATCH,
                  row_blk0=N_CTX // DEC_SEQ, emit_state=False, gps=1)

    x1 = _merge(attn_c, attn_l, y_c, y_l, proj, xp, xl, mod, row(ssd_norm[l]), w_attn_o[l].astype(BF16),
                w_ssd_o[l].astype(BF16), w_out[l].astype(BF16), row(ln1_g[l]), row(ln1_b[l]))
    out_p, out_l = _mlp(x1, mod, w_mlp1[l], row(b_mlp1[l]), w_mlp2[l], row(b_mlp2[l]),
                        row(ln2_g[l]), row(ln2_b[l]))

    y_p = out_p.reshape(BATCH, SEQ, D_MODEL)
    y_s = out_l.reshape(DEC_BATCH, DEC_SEQ, D_MODEL)
    new_state = h_new.reshape(BATCH, DEPTH, 2, H_SSD, SSD_P, SSD_N)
    return (y_p, y_s, k_new, v_new, new_state)
```
